```python
import math
import jax, jax.numpy as jnp
from jax import lax
import numpy as np


D_MODEL = 2048
BATCH = 4
SEQ = 2048
DEPTH = 2
DEC_BATCH = 128
DEC_SEQ = 8
PAST_LEN = 2048
PAGE_SIZE = 128

HEAD_DIM = 128
SB_HEADS = 8
SB_WIDTH = SB_HEADS * HEAD_DIM
SB_BLOCK = 128
SB_BIAS_INIT = -6.0
CONV_WIDTH = D_MODEL // 2
CONV_B_K = 3
LRU_WIDTH = D_MODEL
LRU_BLOCKS = 8
LRU_BLOCK = LRU_WIDTH // LRU_BLOCKS
CONV_C_K = 4
LRU_C = 8.0
N_EVEN = (DEPTH + 1) // 2
N_ODD = DEPTH // 2
ALPHA = (2.0 * DEPTH) ** 0.25
BETA_INIT = (8.0 * DEPTH) ** -0.25
LN_EPS = 1e-5
EVEN_IN = 4 * SB_WIDTH + 4 * CONV_WIDTH
EVEN_SPLITS = (SB_WIDTH, 2 * SB_WIDTH, 3 * SB_WIDTH, 4 * SB_WIDTH,
               4 * SB_WIDTH + CONV_WIDTH, 4 * SB_WIDTH + 2 * CONV_WIDTH, 4 * SB_WIDTH + 3 * CONV_WIDTH)
ODD_IN = 2 * LRU_WIDTH

kernel_name = 'sb_shortconv_rglru_hybrid_step'


def _layernorm(x, g, b):
    xf = x.astype(jnp.float32)
    mu = jnp.mean(xf, axis=-1, keepdims=True)
    var = jnp.mean(jnp.square(xf - mu), axis=-1, keepdims=True)
    return ((xf - mu) * lax.rsqrt(var + LN_EPS) * g + b).astype(x.dtype)


def _modulation(c, w_ada, b_ada):
    m = jax.nn.silu(c) @ w_ada + b_ada
    shift, scale, gate = jnp.split(m, 3, axis=-1)
    return shift[:, None], scale[:, None], gate[:, None]


def _causal_depthwise_conv(x, buf, w):
    K = w.shape[0]
    T = x.shape[1]
    xp = jnp.concatenate([buf.astype(x.dtype), x], axis=1)
    y = w[0] * xp[:, :T]
    for j in range(1, K):
        y = y + w[j] * xp[:, j:j + T]
    return y, xp[:, -(K - 1):]


def _stick_breaking_attention(q, k, v, bias):
    n_q = q.shape[1]
    n_k = k.shape[1]
    offset = n_k - n_q
    qb = math.gcd(SB_BLOCK, n_q)
    scale = 1.0 / math.sqrt(q.shape[-1])
    bias_f = bias.astype(jnp.float32)[None, :, None, None]
    outs = []
    for i in range(n_q // qb):
        end = offset + (i + 1) * qb
        q_i = q[:, i * qb:(i + 1) * qb]
        k_i = k[:, :end]
        v_i = v[:, :end]
        z = jnp.einsum('bqhd,bkhd->bhqk', q_i, k_i).astype(jnp.float32) * scale + bias_f
        q_pos = offset + i * qb + jnp.arange(qb)
        k_pos = jnp.arange(end)
        mask = k_pos[None, :] < q_pos[:, None]
        log_keep = jnp.where(mask, jax.nn.log_sigmoid(-z), 0.0)
        log_after = lax.cumsum(log_keep, axis=3, reverse=True) - log_keep
        w = jnp.where(mask, jnp.exp(jax.nn.log_sigmoid(z) + log_after), 0.0)
        outs.append(jnp.einsum('bhqk,bkhd->bqhd', w.astype(v.dtype), v_i))
    return jnp.concatenate(outs, axis=1)


def _linear_combine(left, right):
    a_l, b_l = left
    a_r, b_r = right
    return a_l * a_r, a_r * b_l + b_r


def _even_layer(x, c, conv_buf, k_past, v_past, w_ada, b_ada, w_in, sb_bias, w_conv, w_out, ln_g, ln_b):
    n_b, T, _ = x.shape
    shift, scale, gate = _modulation(c, w_ada, b_ada)
    u = x * (1.0 + scale) + shift
    proj = u @ w_in
    q, k, v, z_a, b_g, c_g, x_in, z_b = jnp.split(proj, EVEN_SPLITS, axis=-1)
    q = q.reshape(n_b, T, SB_HEADS, HEAD_DIM)
    k = k.reshape(n_b, T, SB_HEADS, HEAD_DIM)
    v = v.reshape(n_b, T, SB_HEADS, HEAD_DIM)
    if k_past is None:
        k_all, v_all = k, v
    else:
        k_all = jnp.concatenate([k_past.astype(k.dtype), k], axis=1)
        v_all = jnp.concatenate([v_past.astype(v.dtype), v], axis=1)
    o_a = _stick_breaking_attention(q, k_all, v_all, sb_bias).reshape(n_b, T, SB_WIDTH) * jax.nn.silu(z_a)
    if conv_buf is None:
        conv_buf = jnp.zeros((n_b, CONV_B_K - 1, CONV_WIDTH), x.dtype)
    conv_out, new_buf = _causal_depthwise_conv(c_g * x_in, conv_buf, w_conv)
    o_b = b_g * conv_out * jax.nn.silu(z_b)
    mix = jnp.concatenate([o_a, o_b], axis=-1) @ w_out
    y = _layernorm(ALPHA * x + (1.0 + gate) * mix, ln_g, ln_b)
    return y, k, v, new_buf


def _odd_layer(x, c, conv_buf, h0, w_ada, b_ada, w_in, w_conv, b_conv, w_gate_a, b_gate_a,
               w_gate_x, b_gate_x, lam, w_out, ln_g, ln_b):
    n_b, T, _ = x.shape
    shift, scale, gate = _modulation(c, w_ada, b_ada)
    u = x * (1.0 + scale) + shift
    x_r, z = jnp.split(u @ w_in, 2, axis=-1)
    if conv_buf is None:
        conv_buf = jnp.zeros((n_b, CONV_C_K - 1, LRU_WIDTH), x.dtype)
    if h0 is None:
        h0 = jnp.zeros((n_b, LRU_WIDTH), x.dtype)
    xc, new_buf = _causal_depthwise_conv(x_r, conv_buf, w_conv)
    xc = xc + b_conv
    xb = xc.reshape(n_b, T, LRU_BLOCKS, LRU_BLOCK)
    g_a = jnp.einsum('btnc,ncd->btnd', xb, w_gate_a).reshape(n_b, T, LRU_WIDTH) + b_gate_a
    g_x = jnp.einsum('btnc,ncd->btnd', xb, w_gate_x).reshape(n_b, T, LRU_WIDTH) + b_gate_x
    log_a = -LRU_C * jax.nn.sigmoid(g_a.astype(jnp.float32)) * jax.nn.softplus(-lam.astype(jnp.float32))
    a = jnp.exp(log_a)
    b_t = jnp.sqrt(-jnp.expm1(2.0 * log_a)) * jax.nn.sigmoid(g_x.astype(jnp.float32)) * xc.astype(jnp.float32)
    b_t = b_t.at[:, 0].add(a[:, 0] * h0.astype(jnp.float32))
    _, h = lax.associative_scan(_linear_combine, (a, b_t), axis=1)
    o = h.astype(x.dtype) * jax.nn.silu(z)
    mix = o @ w_out
    y = _layernorm(ALPHA * x + (1.0 + gate) * mix, ln_g, ln_b)
    return y, new_buf, h[:, -1].astype(x.dtype)


def setup_inputs(seed: int = 0) -> dict:
    key = jax.random.key(seed)
    ks = iter(jax.random.split(key, 40))

    def nrm(shape, s):
        return jax.random.normal(next(ks), shape, jnp.float32) * s

    n_pages = PAST_LEN // PAGE_SIZE
    n_used = DEC_BATCH * n_pages
    n_phys = n_used + max(1, n_used // 4)
    d_inv = D_MODEL ** -0.5
    x_prompt = nrm((BATCH, SEQ, D_MODEL), 1.0)
    x_sample = nrm((DEC_BATCH, DEC_SEQ, D_MODEL), 1.0)
    c_prompt = nrm((BATCH, D_MODEL), 1.0)
    c_sample = nrm((DEC_BATCH, D_MODEL), 1.0)
    cache_k = nrm((N_EVEN, n_phys, PAGE_SIZE, SB_HEADS, HEAD_DIM), 1.0)
    cache_v = nrm((N_EVEN, n_phys, PAGE_SIZE, SB_HEADS, HEAD_DIM), 1.0)
    state_conv_b = nrm((N_EVEN, DEC_BATCH, CONV_B_K - 1, CONV_WIDTH), 1.0)
    state_conv_c = nrm((N_ODD, DEC_BATCH, CONV_C_K - 1, LRU_WIDTH), 1.0)
    state_h = nrm((N_ODD, DEC_BATCH, LRU_WIDTH), 0.5)
    page_table = jax.random.permutation(next(ks), n_phys)[:n_used].reshape(DEC_BATCH, n_pages).astype(jnp.int32)
    we_ada = nrm((N_EVEN, D_MODEL, 3 * D_MODEL), 0.2 * d_inv)
    be_ada = nrm((N_EVEN, 3 * D_MODEL), 0.01)
    we_in = nrm((N_EVEN, D_MODEL, EVEN_IN), d_inv)
    we_sb_bias = SB_BIAS_INIT + nrm((N_EVEN, SB_HEADS), 0.1)
    we_conv = nrm((N_EVEN, CONV_B_K, CONV_WIDTH), CONV_B_K ** -0.5)
    we_out = nrm((N_EVEN, SB_WIDTH + CONV_WIDTH, D_MODEL), BETA_INIT * (SB_WIDTH + CONV_WIDTH) ** -0.5)
    ge_ln = 1.0 + nrm((N_EVEN, D_MODEL), 0.02)
    be_ln = nrm((N_EVEN, D_MODEL), 0.02)
    wo_ada = nrm((N_ODD, D_MODEL, 3 * D_MODEL), 0.2 * d_inv)
    bo_ada = nrm((N_ODD, 3 * D_MODEL), 0.01)
    wo_in = nrm((N_ODD, D_MODEL, ODD_IN), d_inv)
    wo_conv = nrm((N_ODD, CONV_C_K, LRU_WIDTH), CONV_C_K ** -0.5)
    bo_conv = nrm((N_ODD, LRU_WIDTH), 0.01)
    wo_gate_a = nrm((N_ODD, LRU_BLOCKS, LRU_BLOCK, LRU_BLOCK), LRU_BLOCK ** -0.5)
    bo_gate_a = nrm((N_ODD, LRU_WIDTH), 0.01)
    wo_gate_x = nrm((N_ODD, LRU_BLOCKS, LRU_BLOCK, LRU_BLOCK), LRU_BLOCK ** -0.5)
    bo_gate_x = nrm((N_ODD, LRU_WIDTH), 0.01)
    a0 = jax.random.uniform(next(ks), (N_ODD, LRU_WIDTH), jnp.float32, 0.9, 0.999)
    root = a0 ** (1.0 / LRU_C)
    wo_lambda = jnp.log(root) - jnp.log1p(-root)
    wo_out = nrm((N_ODD, LRU_WIDTH, D_MODEL), BETA_INIT * LRU_WIDTH ** -0.5)
    go_ln = 1.0 + nrm((N_ODD, D_MODEL), 0.02)
    bo_ln = nrm((N_ODD, D_MODEL), 0.02)
    return {'x_prompt': x_prompt, 'x_sample': x_sample, 'c_prompt': c_prompt, 'c_sample': c_sample,
            'cache_k': cache_k, 'cache_v': cache_v, 'state_conv_b': state_conv_b,
            'state_conv_c': state_conv_c, 'state_h': state_h, 'page_table': page_table,
            'we_ada': we_ada, 'be_ada': be_ada, 'we_in': we_in, 'we_sb_bias': we_sb_bias,
            'we_conv': we_conv, 'we_out': we_out, 'ge_ln': ge_ln, 'be_ln': be_ln,
            'wo_ada': wo_ada, 'bo_ada': bo_ada, 'wo_in': wo_in, 'wo_conv': wo_conv, 'bo_conv': bo_conv,
            'wo_gate_a': wo_gate_a, 'bo_gate_a': bo_gate_a, 'wo_gate_x': wo_gate_x, 'bo_gate_x': bo_gate_x,
            'wo_lambda': wo_lambda, 'wo_out': wo_out, 'go_ln': go_ln, 'bo_ln': bo_ln}


def reference(x_prompt, x_sample, c_prompt, c_sample, cache_k, cache_v, state_conv_b, state_conv_c,
              state_h, page_table, we_ada, be_ada, we_in, we_sb_bias, we_conv, we_out, ge_ln, be_ln,
              wo_ada, bo_ada, wo_in, wo_conv, bo_conv, wo_gate_a, bo_gate_a, wo_gate_x, bo_gate_x,
              wo_lambda, wo_out, go_ln, bo_ln):
    n_seq = page_table.shape[0]
    yp, ys = x_prompt, x_sample
    kp_l, vp_l, bp_l, ks_l, vs_l, bs_l = [], [], [], [], [], []
    cp_l, hp_l, cs_l, hs_l = [], [], [], []
    for layer in range(DEPTH):
        i = layer // 2
        if layer % 2 == 0:
            ew = (we_ada[i], be_ada[i], we_in[i], we_sb_bias[i], we_conv[i], we_out[i], ge_ln[i], be_ln[i])
            yp, kp, vp, bp = _even_layer(yp, c_prompt, None, None, None, *ew)
            k_past = cache_k[i][page_table].reshape(n_seq, -1, SB_HEADS, HEAD_DIM)
            v_past = cache_v[i][page_table].reshape(n_seq, -1, SB_HEADS, HEAD_DIM)
            ys, ks_, vs_, bs_ = _even_layer(ys, c_sample, state_conv_b[i], k_past, v_past, *ew)
            kp_l.append(kp); vp_l.append(vp); bp_l.append(bp)
            ks_l.append(ks_); vs_l.append(vs_); bs_l.append(bs_)
        else:
            ow = (wo_ada[i], bo_ada[i], wo_in[i], wo_conv[i], bo_conv[i], wo_gate_a[i], bo_gate_a[i],
                  wo_gate_x[i], bo_gate_x[i], wo_lambda[i], wo_out[i], go_ln[i], bo_ln[i])
            yp, cp, hp = _odd_layer(yp, c_prompt, None, None, *ow)
            ys, cs_, hs_ = _odd_layer(ys, c_sample, state_conv_c[i], state_h[i], *ow)
            cp_l.append(cp); hp_l.append(hp); cs_l.append(cs_); hs_l.append(hs_)
    k_prompt = jnp.stack(kp_l)
    v_prompt = jnp.stack(vp_l)
    conv_b_prompt = jnp.stack(bp_l)
    conv_c_prompt = jnp.stack(cp_l)
    h_prompt = jnp.stack(hp_l)
    k_sample = jnp.stack(ks_l)
    v_sample = jnp.stack(vs_l)
    conv_b_sample = jnp.stack(bs_l)
    conv_c_sample = jnp.stack(cs_l)
    h_sample = jnp.stack(hs_l)
    return (yp, ys, k_prompt, v_prompt, conv_b_prompt, conv_c_prompt, h_prompt,
            k_sample, v_sample, conv_b_sample, conv_c_sample, h_sample)
```

```python
import functools
import math

import jax
import jax.numpy as jnp
from jax import lax
from jax.experimental import pallas as pl
from jax.experimental.pallas import tpu as pltpu

F32 = jnp.float32
BF16 = jnp.bfloat16

HEAD_DIM = 128
SB_BLOCK = 256
LRU_C = 8.0
LN_EPS = 1e-5
SUBLANES = 8
VMEM_LIMIT = 56 * 1024 * 1024


def _cparams(n_axes):
    return pltpu.CompilerParams(dimension_semantics=("arbitrary",) * n_axes, vmem_limit_bytes=VMEM_LIMIT)


def _silu(x):
    return x * jax.nn.sigmoid(x)


def _softplus(x):
    return jnp.maximum(x, 0.0) + jnp.log(1.0 + jnp.exp(-jnp.abs(x)))


def _idiv(x, n):
    assert n & (n - 1) == 0
    return x >> (n.bit_length() - 1)


def _imod(x, n):
    assert n & (n - 1) == 0
    return x & (n - 1)


def _dot(a, b):
    return jnp.dot(a, b, preferred_element_type=F32)


def _dot_nt(a, b):
    return lax.dot_general(a, b, (((1,), (1,)), ((), ())), preferred_element_type=F32)


def _modulate(x, scale, shift):
    gm = scale.shape[0]
    if gm == 1:
        return x * (1.0 + scale) + shift
    rows, d = x.shape
    x3 = x.reshape(rows // gm, gm, d)
    return (x3 * (1.0 + scale)[None] + shift[None]).reshape(rows, d)


def _gate_mix(mix, gate):
    gm = gate.shape[0]
    if gm == 1:
        return mix * (1.0 + gate)
    rows, d = mix.shape
    return (mix.reshape(rows // gm, gm, d) * (1.0 + gate)[None]).reshape(rows, d)


def _shift_rows(cur, prev, d):
    tm, p = cur.shape[0], prev.shape[0]
    if d % SUBLANES == 0:
        if d >= tm:
            return prev[p - d:p - d + tm]
        return jnp.concatenate([prev[p - d:], cur[:tm - d]], axis=0)
    assert d < SUBLANES <= p
    rolled = pltpu.roll(cur, d, 0)
    prev_rolled = pltpu.roll(prev[p - SUBLANES:], d, 0)
    row = lax.broadcasted_iota(jnp.int32, prev_rolled.shape, 0)
    head = jnp.where(row < d, prev_rolled, rolled[:SUBLANES])
    return jnp.concatenate([head, rolled[SUBLANES:]], axis=0)


def _next_prev(cur, prev):
    tm, p = cur.shape[0], prev.shape[0]
    if tm >= p:
        return cur[tm - p:]
    return jnp.concatenate([prev[tm:], cur], axis=0)


def _ada_kernel(c_ref, w_ref, b_ref, o_ref):
    s = _silu(c_ref[...]).astype(BF16)
    o_ref[...] = _dot(s, w_ref[...].astype(BF16)) + b_ref[...]


def _ada(c_all, w_ada, b_ada, tn=1024):
    rows, d = c_all.shape
    n = w_ada.shape[1]
    return pl.pallas_call(
        _ada_kernel,
        grid=(n // tn,),
        in_specs=[pl.BlockSpec((rows, d), lambda j: (0, 0)),
                  pl.BlockSpec((d, tn), lambda j: (0, j)),
                  pl.BlockSpec((1, tn), lambda j: (0, j))],
        out_specs=pl.BlockSpec((rows, tn), lambda j: (0, j)),
        out_shape=jax.ShapeDtypeStruct((rows, n), F32),
        compiler_params=_cparams(1),
        name="ada",
    )(c_all, w_ada, b_ada.reshape(1, n))


def _inproj0_kernel(x_ref, shift_ref, scale_ref, state_ref, wq, wk, wv, wza, wbg, wcg, wxi, wzb, wconv_ref,
                    q_ref, k_ref, v_ref, za_ref, ob_ref, tail_ref, u_scr, carry_scr, *, stride, tiles_per_seq):
    i = pl.program_id(0)
    j = pl.program_id(1)

    @pl.when(j == 0)
    def _():
        u_scr[...] = _modulate(x_ref[...], scale_ref[...], shift_ref[...]).astype(BF16)

    @pl.when(i % tiles_per_seq == 0)
    def _():
        carry_scr[j] = state_ref[...]

    u = u_scr[...]
    q_ref[...] = _dot(u, wq[...]).astype(q_ref.dtype)
    k_ref[...] = _dot(u, wk[...])
    v_ref[...] = _dot(u, wv[...])
    za_ref[...] = _dot(u, wza[...])
    ci = _dot(u, wcg[...]) * _dot(u, wxi[...])
    prev = carry_scr[j]
    wc = wconv_ref[...]
    conv = wc[2:3] * ci + wc[1:2] * _shift_rows(ci, prev, stride) + wc[0:1] * _shift_rows(ci, prev, 2 * stride)
    ob_ref[...] = (_dot(u, wbg[...]) * conv * _silu(_dot(u, wzb[...]))).astype(ob_ref.dtype)
    new_prev = _next_prev(ci, prev)
    carry_scr[j] = new_prev
    tail_ref[...] = new_prev


def _inproj0(x, shift, scale, state, w_in, w_conv, *, tm, tn, stride, tiles_per_seq, q_dtype):
    rows, d = x.shape
    c = w_conv.shape[1]
    p = state.shape[1]
    n_i, n_j = rows // tm, c // tn
    gm = shift.shape[-2]
    if shift.ndim == 3:
        mod_spec = pl.BlockSpec((None, gm, d), lambda i, j: (i // tiles_per_seq, 0, 0))
    else:
        mod_spec = pl.BlockSpec((gm, d), lambda i, j: (0, 0))

    def w_spec(g):
        return pl.BlockSpec((d, tn), lambda i, j: (0, g * n_j + j))

    act_spec = pl.BlockSpec((tm, tn), lambda i, j: (i, j))
    outs = pl.pallas_call(
        functools.partial(_inproj0_kernel, stride=stride, tiles_per_seq=tiles_per_seq),
        grid=(n_i, n_j),
        in_specs=[pl.BlockSpec((tm, d), lambda i, j: (i, 0)), mod_spec, mod_spec,
                  pl.BlockSpec((None, p, tn), lambda i, j: (i // tiles_per_seq, 0, j))]
                 + [w_spec(g) for g in range(8)]
                 + [pl.BlockSpec((w_conv.shape[0], tn), lambda i, j: (0, j))],
        out_specs=[act_spec] * 5 + [pl.BlockSpec((None, p, tn), lambda i, j: (i, 0, j))],
        out_shape=[jax.ShapeDtypeStruct((rows, c), q_dtype)] + [jax.ShapeDtypeStruct((rows, c), F32)] * 3
                  + [jax.ShapeDtypeStruct((rows, c), BF16), jax.ShapeDtypeStruct((n_i, p, c), F32)],
        scratch_shapes=[pltpu.VMEM((tm, d), BF16), pltpu.VMEM((n_j, p, tn), F32)],
        compiler_params=_cparams(2),
        name="inproj0",
    )(x, shift, scale, state, *([w_in] * 8), w_conv)
    return outs


def _sb_weights(z, carry, tri, mask):
    log_keep = -_softplus(z)
    if mask is not None:
        log_keep = jnp.where(mask, log_keep, 0.0)
    hi = log_keep.astype(BF16)
    lo = (log_keep - hi.astype(F32)).astype(BF16)
    incl = _dot(hi, tri) + _dot(lo, tri)
    w = jnp.exp(z + incl + carry)
    if mask is not None:
        w = jnp.where(mask, w, 0.0)
    return w, carry + jnp.sum(log_keep, axis=-1, keepdims=True)


def _tri(t):
    r = lax.broadcasted_iota(jnp.int32, (t, t), 0)
    c = lax.broadcasted_iota(jnp.int32, (t, t), 1)
    return jnp.where(r >= c, 1.0, 0.0).astype(BF16)


def _attn_prompt_kernel(bias_ref, q_ref, k_ref, v_ref, za_ref, o_ref, *, t, scale):
    h = pl.program_id(1)
    qi = pl.program_id(2)
    bias = bias_ref[0, h]
    q = q_ref[...]
    tri = _tri(t)
    r = lax.broadcasted_iota(jnp.int32, (t, t), 0)
    c = lax.broadcasted_iota(jnp.int32, (t, t), 1)

    def block(kj, carry, acc, mask):
        start = pl.multiple_of(kj * t, t)
        kb = k_ref[pl.ds(start, t), :].astype(BF16)
        vb = v_ref[pl.ds(start, t), :].astype(BF16)
        z = _dot_nt(q, kb) * scale + bias
        w, carry = _sb_weights(z, carry, tri, mask)
        return carry, acc + _dot(w.astype(BF16), vb)

    carry = jnp.zeros((t, 1), F32)
    acc = jnp.zeros((t, HEAD_DIM), F32)
    carry, acc = block(qi, carry, acc, c < r)
    carry, acc = lax.fori_loop(0, qi, lambda s, ca: block(qi - 1 - s, ca[0], ca[1], None), (carry, acc))
    o_ref[...] = (acc * _silu(za_ref[...])).astype(o_ref.dtype)


def _attn_prompt(bias, q, k, v, za, *, n_batch, seq):
    rows, width = k.shape
    n_heads = width // HEAD_DIM
    t = SB_BLOCK
    nq = seq // t
    q_spec = pl.BlockSpec((t, HEAD_DIM), lambda b, h, qi: (b * nq + qi, h))
    kv_spec = pl.BlockSpec((seq, HEAD_DIM), lambda b, h, qi: (b, h))
    return pl.pallas_call(
        functools.partial(_attn_prompt_kernel, t=t, scale=1.0 / math.sqrt(HEAD_DIM)),
        grid=(n_batch, n_heads, nq),
        in_specs=[pl.BlockSpec(memory_space=pltpu.SMEM), q_spec, kv_spec, kv_spec, q_spec],
        out_specs=q_spec,
        out_shape=jax.ShapeDtypeStruct((rows, width), BF16),
        compiler_params=_cparams(3),
        name="attn_prompt",
    )(bias, q, k, v, za)


def _attn_sample_kernel(pt_ref, bias_ref, q_ref, kn_ref, vn_ref, za_ref, *rest, n_pages, page, n_heads, scale):
    del pt_ref
    k_pages = rest[:n_pages]
    v_pages = rest[n_pages:2 * n_pages]
    o_ref, k_all, v_all = rest[2 * n_pages:]
    t_new, width = kn_ref.shape
    past = n_pages * page
    n_blk = n_pages + 1
    for p in range(n_pages):
        k_all[p * page:(p + 1) * page, :] = k_pages[p][...].astype(BF16)
        v_all[p * page:(p + 1) * page, :] = v_pages[p][...].astype(BF16)
    pad = jnp.zeros((page - t_new, width), F32)
    k_all[past:past + page, :] = jnp.concatenate([kn_ref[...], pad], axis=0).astype(BF16)
    v_all[past:past + page, :] = jnp.concatenate([vn_ref[...], pad], axis=0).astype(BF16)

    m = n_heads * t_new
    row = lax.broadcasted_iota(jnp.int32, (m, width), 0)
    lane = lax.broadcasted_iota(jnp.int32, (m, width), 1)
    q_tiled = jnp.concatenate([q_ref[...]] * n_heads, axis=0)
    q_bd = jnp.where(_idiv(row, t_new) == _idiv(lane, HEAD_DIM), q_tiled, 0.0).astype(BF16)
    row1 = lax.broadcasted_iota(jnp.int32, (m, 1), 0)
    bias = jnp.zeros((m, 1), F32)
    for h in range(n_heads):
        bias = jnp.where(_idiv(row1, t_new) == h, bias_ref[0, h], bias)
    z_all = _dot_nt(q_bd, k_all[...]) * scale + bias

    tri = _tri(page)
    rb = lax.broadcasted_iota(jnp.int32, (m, page), 0)
    cb = lax.broadcasted_iota(jnp.int32, (m, page), 1)
    new_mask = cb < _imod(rb, t_new)
    carry = jnp.zeros((m, 1), F32)
    w_blocks = [None] * n_blk
    for j in range(n_blk - 1, -1, -1):
        z = z_all[:, j * page:(j + 1) * page]
        w, carry = _sb_weights(z, carry, tri, new_mask if j == n_blk - 1 else None)
        w_blocks[j] = w.astype(BF16)
    out = _dot(jnp.concatenate(w_blocks, axis=1), v_all[...])
    o = jnp.concatenate([out[h * t_new:(h + 1) * t_new, h * HEAD_DIM:(h + 1) * HEAD_DIM]
                         for h in range(n_heads)], axis=1)
    o_ref[...] = (o * _silu(za_ref[...])).astype(o_ref.dtype)


def _attn_sample(page_table, bias, q, k_new, v_new, za, cache_k, cache_v):
    n_seq, t_new, width = q.shape
    n_pages = page_table.shape[1]
    page = cache_k.shape[1]
    n_heads = width // HEAD_DIM
    seq_spec = pl.BlockSpec((None, t_new, width), lambda n, pt: (n, 0, 0))

    def page_spec(p):
        return pl.BlockSpec((None, page, width), lambda n, pt: (pt[n, p], 0, 0))

    grid_spec = pltpu.PrefetchScalarGridSpec(
        num_scalar_prefetch=1,
        grid=(n_seq,),
        in_specs=[pl.BlockSpec(memory_space=pltpu.SMEM), seq_spec, seq_spec, seq_spec, seq_spec]
                 + [page_spec(p) for p in range(n_pages)] * 2,
        out_specs=seq_spec,
        scratch_shapes=[pltpu.VMEM(((n_pages + 1) * page, width), BF16)] * 2,
    )
    return pl.pallas_call(
        functools.partial(_attn_sample_kernel, n_pages=n_pages, page=page, n_heads=n_heads,
                          scale=1.0 / math.sqrt(HEAD_DIM)),
        grid_spec=grid_spec,
        out_shape=jax.ShapeDtypeStruct((n_seq, t_new, width), F32),
        compiler_params=_cparams(1),
        name="attn_sample",
    )(page_table, bias, q, k_new, v_new, za, *([cache_k] * n_pages), *([cache_v] * n_pages))


def _outproj_ln_kernel(*refs, n_in, alpha):
    x_ref, gate_ref = refs[0], refs[1]
    acts = refs[2:2 + n_in]
    ws = refs[2 + n_in:2 + 2 * n_in]
    g_ref, b_ref, y_ref = refs[2 + 2 * n_in:]
    mix = _dot(acts[0][...].astype(BF16), ws[0][...])
    for a, w in zip(acts[1:], ws[1:]):
        mix = mix + _dot(a[...].astype(BF16), w[...])
    r = alpha * x_ref[...] + _gate_mix(mix, gate_ref[...])
    mu = jnp.mean(r, axis=-1, keepdims=True)
    rc = r - mu
    var = jnp.mean(rc * rc, axis=-1, keepdims=True)
    y_ref[...] = rc * lax.rsqrt(var + LN_EPS) * g_ref[...] + b_ref[...]


def _outproj_ln(x, gate, acts, w_out, ln_g, ln_b, *, tm, tiles_per_seq, alpha):
    rows, d = x.shape
    gm = gate.shape[-2]
    if gate.ndim == 3:
        mod_spec = pl.BlockSpec((None, gm, d), lambda i: (i // tiles_per_seq, 0, 0))
    else:
        mod_spec = pl.BlockSpec((gm, d), lambda i: (0, 0))
    n_in = len(acts)
    kw = w_out.shape[0] // n_in
    once = pl.Buffered(1)
    w_specs = [pl.BlockSpec((kw, d), functools.partial(lambda i, g: (g, 0), g=g), pipeline_mode=once)
               for g in range(n_in)]
    vec_spec = pl.BlockSpec((1, d), lambda i: (0, 0))
    return pl.pallas_call(
        functools.partial(_outproj_ln_kernel, n_in=n_in, alpha=alpha),
        grid=(rows // tm,),
        in_specs=[pl.BlockSpec((tm, d), lambda i: (i, 0)), mod_spec]
                 + [pl.BlockSpec((tm, kw), lambda i: (i, 0))] * n_in + w_specs + [vec_spec, vec_spec],
        out_specs=pl.BlockSpec((tm, d), lambda i: (i, 0)),
        out_shape=jax.ShapeDtypeStruct((rows, d), F32),
        compiler_params=_cparams(1),
        name="outproj_ln",
    )(x, gate, *acts, *([w_out] * n_in), ln_g.reshape(1, d), ln_b.reshape(1, d))


def _scan_rows(a_scr, b_scr, h_scr, h0):
    def body(g, h):
        r0 = pl.multiple_of(g * SUBLANES, SUBLANES)
        a8 = a_scr[pl.ds(r0, SUBLANES), :]
        b8 = b_scr[pl.ds(r0, SUBLANES), :]
        out = []
        for r in range(SUBLANES):
            h = a8[r:r + 1] * h + b8[r:r + 1]
            out.append(h)
        h_scr[pl.ds(r0, SUBLANES), :] = jnp.concatenate(out, axis=0)
        return h
    return lax.fori_loop(0, a_scr.shape[0] // SUBLANES, body, h0)


def _lru_kernel(x_ref, shift_ref, scale_ref, state_ref, h0_ref, wx_ref, wz_ref, wconv_ref, bconv_ref,
                wga_ref, wgx_ref, bga_ref, bgx_ref, lam_ref,
                o_ref, tail_ref, hlast_ref, prev_scr, hcar_scr, a_scr, b_scr, h_scr, *, stride, tiles_per_seq):
    i = pl.program_id(0)

    @pl.when(i % tiles_per_seq == 0)
    def _():
        prev_scr[...] = state_ref[...]
        hcar_scr[...] = h0_ref[...]

    u = _modulate(x_ref[...], scale_ref[...], shift_ref[...]).astype(BF16)
    xr = _dot(u, wx_ref[...])
    prev = prev_scr[...]
    wc = wconv_ref[...]
    xc = wc[3:4] * xr + bconv_ref[...]
    for kk in range(1, 4):
        xc = xc + wc[3 - kk:4 - kk] * _shift_rows(xr, prev, kk * stride)
    new_prev = _next_prev(xr, prev)
    prev_scr[...] = new_prev
    tail_ref[...] = new_prev

    xcb = xc.astype(BF16)
    n_blocks, blk, _ = wga_ref.shape
    ga = jnp.concatenate([_dot(xcb[:, n * blk:(n + 1) * blk], wga_ref[n]) for n in range(n_blocks)], axis=1)
    gx = jnp.concatenate([_dot(xcb[:, n * blk:(n + 1) * blk], wgx_ref[n]) for n in range(n_blocks)], axis=1)
    lam = lam_ref[...]
    neg_log_sig = jnp.maximum(-lam, 0.0) + jnp.log1p(jnp.exp(-jnp.abs(lam)))
    log_a = (-LRU_C) * jax.nn.sigmoid(ga + bga_ref[...]) * neg_log_sig
    a = jnp.exp(log_a)
    bt = jnp.sqrt(jnp.tanh(-log_a) * (1.0 + a * a)) * jax.nn.sigmoid(gx + bgx_ref[...]) * xc

    if stride == x_ref.shape[0]:
        h = a * hcar_scr[...] + bt
        hcar_scr[...] = h
    else:
        a_scr[...] = a
        b_scr[...] = bt
        hcar_scr[...] = _scan_rows(a_scr, b_scr, h_scr, hcar_scr[...])
        h = h_scr[...]
    hlast_ref[...] = hcar_scr[...]
    z = _dot(u, wz_ref[...])
    o_ref[...] = (h * _silu(z)).astype(o_ref.dtype)


def _lru(x, shift, scale, state, h0, w_in, w_conv, b_conv, w_ga, w_gx, b_ga, b_gx, lam, *, tm, stride,
         tiles_per_seq):
    rows, d = x.shape
    p = state.shape[1]
    hr = h0.shape[1]
    n_i = rows // tm
    gm = shift.shape[-2]
    if shift.ndim == 3:
        mod_spec = pl.BlockSpec((None, gm, d), lambda i: (i // tiles_per_seq, 0, 0))
    else:
        mod_spec = pl.BlockSpec((gm, d), lambda i: (0, 0))
    once = pl.Buffered(1)
    vec_spec = pl.BlockSpec((1, d), lambda i: (0, 0))
    gate_spec = pl.BlockSpec(w_ga.shape, lambda i: (0, 0, 0), pipeline_mode=once)
    scan_scr = (tm, d) if stride != tm else (SUBLANES, 128)
    return pl.pallas_call(
        functools.partial(_lru_kernel, stride=stride, tiles_per_seq=tiles_per_seq),
        grid=(n_i,),
        in_specs=[pl.BlockSpec((tm, d), lambda i: (i, 0)), mod_spec, mod_spec,
                  pl.BlockSpec((None, p, d), lambda i: (i // tiles_per_seq, 0, 0)),
                  pl.BlockSpec((None, hr, d), lambda i: (i // tiles_per_seq, 0, 0)),
                  pl.BlockSpec((d, d), lambda i: (0, 0), pipeline_mode=once),
                  pl.BlockSpec((d, d), lambda i: (0, 1), pipeline_mode=once),
                  pl.BlockSpec((w_conv.shape[0], d), lambda i: (0, 0)), vec_spec,
                  gate_spec, gate_spec, vec_spec, vec_spec, vec_spec],
        out_specs=[pl.BlockSpec((tm, d), lambda i: (i, 0)),
                   pl.BlockSpec((None, p, d), lambda i: (i, 0, 0)),
                   pl.BlockSpec((None, hr, d), lambda i: (i, 0, 0))],
        out_shape=[jax.ShapeDtypeStruct((rows, d), BF16), jax.ShapeDtypeStruct((n_i, p, d), F32),
                   jax.ShapeDtypeStruct((n_i, hr, d), F32)],
        scratch_shapes=[pltpu.VMEM((p, d), F32), pltpu.VMEM((hr, d), F32)]
                       + [pltpu.VMEM(scan_scr, F32)] * 3,
        compiler_params=_cparams(1),
        name="lru",
    )(x, shift, scale, state, h0, w_in, w_in, w_conv, b_conv.reshape(1, d), w_ga, w_gx,
      b_ga.reshape(1, d), b_gx.reshape(1, d), lam.reshape(1, d))


def kernel(x_prompt, x_sample, c_prompt, c_sample, cache_k, cache_v, state_conv_b, state_conv_c, state_h, page_table, we_ada, be_ada, we_in, we_sb_bias, we_conv, we_out, ge_ln, be_ln, wo_ada, bo_ada, wo_in, wo_conv, bo_conv, wo_gate_a, bo_gate_a, wo_gate_x, bo_gate_x, wo_lambda, wo_out, go_ln, bo_ln):
    nb, seq, d = x_prompt.shape
    ns, dec, _ = x_sample.shape
    assert we_ada.shape[0] == 1 and wo_ada.shape[0] == 1, "one even and one odd layer"
    n_heads = we_sb_bias.shape[1]
    width = n_heads * HEAD_DIM
    alpha = (2.0 * (we_ada.shape[0] + wo_ada.shape[0])) ** 0.25
    tm = 512

    def to_time_major(a):
        return jnp.swapaxes(a, 0, 1).reshape(a.shape[1] * ns, a.shape[2])

    def to_seq_major(a, t):
        return jnp.swapaxes(a.reshape(t, ns, a.shape[-1]), 0, 1)

    pad = (-(nb + ns)) % SUBLANES
    c_all = jnp.concatenate([c_prompt, c_sample, jnp.zeros((pad, d), F32)], axis=0)
    mods = []
    for w_ada, b_ada in ((we_ada[0], be_ada[0]), (wo_ada[0], bo_ada[0])):
        m = _ada(c_all, w_ada, b_ada)
        parts = [m[:, g * d:(g + 1) * d] for g in range(3)]
        mods.append(([v[:nb].reshape(nb, 1, d) for v in parts], [v[nb:nb + ns] for v in parts]))

    xp = x_prompt.reshape(nb * seq, d)
    xs = to_time_major(x_sample)
    bias = we_sb_bias[0].reshape(1, n_heads)

    (shift_p, scale_p, gate_p), (shift_s, scale_s, gate_s) = mods[0]
    w_in0 = we_in[0].astype(BF16)
    w_out0 = we_out[0].astype(BF16)
    conv_k = we_conv.shape[1]
    qp, kp, vp, zap, obp, tailp = _inproj0(
        xp, shift_p, scale_p, jnp.zeros((nb, SUBLANES, width), F32), w_in0, we_conv[0],
        tm=tm, tn=256, stride=1, tiles_per_seq=seq // tm, q_dtype=BF16)
    oap = _attn_prompt(bias, qp, kp, vp, zap, n_batch=nb, seq=seq)
    yp = _outproj_ln(xp, gate_p, [oap, obp], w_out0, ge_ln[0], be_ln[0], tm=tm, tiles_per_seq=seq // tm,
                     alpha=alpha)
    conv_b_prompt = tailp.reshape(nb, seq // tm, SUBLANES, width)[:, -1, SUBLANES - (conv_k - 1):][None]

    state_b = to_time_major(state_conv_b[0])[None]
    rows_s = ns * dec
    tiles_s = rows_s // tm
    qs, ks, vs, zas, obs, tails = _inproj0(
        xs, shift_s, scale_s, state_b, w_in0, we_conv[0],
        tm=tm, tn=256, stride=ns, tiles_per_seq=tiles_s, q_dtype=F32)
    ks_sm, vs_sm = to_seq_major(ks, dec), to_seq_major(vs, dec)
    n_phys, page = cache_k.shape[1], cache_k.shape[2]
    oas = _attn_sample(page_table, bias, to_seq_major(qs, dec), ks_sm, vs_sm, to_seq_major(zas, dec),
                       cache_k[0].reshape(n_phys, page, width), cache_v[0].reshape(n_phys, page, width))
    ys = _outproj_ln(xs, gate_s, [to_time_major(oas), obs], w_out0, ge_ln[0], be_ln[0], tm=tm,
                     tiles_per_seq=tiles_s, alpha=alpha)
    conv_b_sample = to_seq_major(tails[-1], conv_k - 1)[None]

    (shift_p, scale_p, gate_p), (shift_s, scale_s, gate_s) = mods[1]
    w_in1 = wo_in[0].astype(BF16)
    w_out1 = wo_out[0].astype(BF16)
    w_ga = wo_gate_a[0].astype(BF16)
    w_gx = wo_gate_x[0].astype(BF16)
    conv_k = wo_conv.shape[1]
    lru_w = (w_in1, wo_conv[0], bo_conv[0], w_ga, w_gx, bo_gate_a[0], bo_gate_x[0], wo_lambda[0])
    tm1 = 256
    op, tailp, hp = _lru(yp, shift_p, scale_p, jnp.zeros((nb, SUBLANES, d), F32), jnp.zeros((nb, 1, d), F32),
                         *lru_w, tm=tm1, stride=1, tiles_per_seq=seq // tm1)
    yp = _outproj_ln(yp, gate_p, [op], w_out1, go_ln[0], bo_ln[0], tm=tm, tiles_per_seq=seq // tm, alpha=alpha)
    conv_c_prompt = tailp.reshape(nb, seq // tm1, SUBLANES, d)[:, -1, SUBLANES - (conv_k - 1):][None]
    h_prompt = hp.reshape(nb, seq // tm1, d)[:, -1][None]

    state_c = to_time_major(state_conv_c[0])[None]
    os_, tails, hs = _lru(ys, shift_s, scale_s, state_c, state_h[0][None], *lru_w, tm=ns, stride=ns,
                          tiles_per_seq=dec)
    ys = _outproj_ln(ys, gate_s, [os_], w_out1, go_ln[0], bo_ln[0], tm=tm, tiles_per_seq=tiles_s, alpha=alpha)
    conv_c_sample = to_seq_major(tails[-1], conv_k - 1)[None]
    h_sample = hs[-1][None]

    y_prompt = yp.reshape(nb, seq, d)
    y_sample = to_seq_major(ys, dec)
    k_prompt = kp.reshape(1, nb, seq, n_heads, HEAD_DIM)
    v_prompt = vp.reshape(1, nb, seq, n_heads, HEAD_DIM)
    k_sample = ks_sm.reshape(1, ns, dec, n_heads, HEAD_DIM)
    v_sample = vs_sm.reshape(1, ns, dec, n_heads, HEAD_DIM)
    return (y_prompt, y_sample, k_prompt, v_prompt, conv_b_prompt, conv_c_prompt, h_prompt,
            k_sample, v_sample, conv_b_sample, conv_c_sample, h_sample)
```

```python
import functools
import math

import jax
import jax.numpy as jnp
from jax import lax
from jax.experimental import pallas as pl
from jax.experimental.pallas import tpu as pltpu

F32 = jnp.float32
BF16 = jnp.bfloat16

HEAD_DIM = 128
SB_BLOCK = 256
LRU_C = 8.0
LN_EPS = 1e-5
LOG2E = math.log2(math.e)
Q_SCALE = LOG2E / math.sqrt(HEAD_DIM)
SUBLANES = 8
VMEM_LIMIT = 56 * 1024 * 1024


def _cparams(n_axes):
    return pltpu.CompilerParams(dimension_semantics=("arbitrary",) * n_axes, vmem_limit_bytes=VMEM_LIMIT)


def _silu(x):
    return x * jax.nn.sigmoid(x)


def _idiv(x, n):
    assert n & (n - 1) == 0
    return x >> (n.bit_length() - 1)


def _imod(x, n):
    assert n & (n - 1) == 0
    return x & (n - 1)


def _dot(a, b):
    return jnp.dot(a, b, preferred_element_type=F32)


def _dot_nt(a, b):
    return lax.dot_general(a, b, (((1,), (1,)), ((), ())), preferred_element_type=F32)


def _modulate(x, scale, shift):
    gm = scale.shape[0]
    if gm == 1:
        return x * (1.0 + scale) + shift
    rows, d = x.shape
    x3 = x.reshape(rows // gm, gm, d)
    return (x3 * (1.0 + scale)[None] + shift[None]).reshape(rows, d)


def _gate_mix(mix, gate):
    gm = gate.shape[0]
    if gm == 1:
        return mix * (1.0 + gate)
    rows, d = mix.shape
    return (mix.reshape(rows // gm, gm, d) * (1.0 + gate)[None]).reshape(rows, d)


def _shift_rows(cur, prev, d):
    tm, p = cur.shape[0], prev.shape[0]
    if d % SUBLANES == 0:
        if d >= tm:
            return prev[p - d:p - d + tm]
        return jnp.concatenate([prev[p - d:], cur[:tm - d]], axis=0)
    assert d < SUBLANES <= p
    rolled = pltpu.roll(cur, d, 0)
    prev_rolled = pltpu.roll(prev[p - SUBLANES:], d, 0)
    row = lax.broadcasted_iota(jnp.int32, prev_rolled.shape, 0)
    head = jnp.where(row < d, prev_rolled, rolled[:SUBLANES])
    return jnp.concatenate([head, rolled[SUBLANES:]], axis=0)


def _next_prev(cur, prev):
    tm, p = cur.shape[0], prev.shape[0]
    if tm >= p:
        return cur[tm - p:]
    return jnp.concatenate([prev[tm:], cur], axis=0)


def _ada_kernel(c_ref, w_ref, b_ref, o_ref):
    s = _silu(c_ref[...]).astype(BF16)
    o_ref[...] = _dot(s, w_ref[...].astype(BF16)) + b_ref[...]


def _ada(c_all, w_ada, b_ada, tn=1024):
    rows, d = c_all.shape
    n = w_ada.shape[1]
    return pl.pallas_call(
        _ada_kernel,
        grid=(n // tn,),
        in_specs=[pl.BlockSpec((rows, d), lambda j: (0, 0)),
                  pl.BlockSpec((d, tn), lambda j: (0, j)),
                  pl.BlockSpec((1, tn), lambda j: (0, j))],
        out_specs=pl.BlockSpec((rows, tn), lambda j: (0, j)),
        out_shape=jax.ShapeDtypeStruct((rows, n), F32),
        compiler_params=_cparams(1),
        name="ada",
    )(c_all, w_ada, b_ada.reshape(1, n))


def _inproj0_kernel(x_ref, shift_ref, scale_ref, state_ref, wq, wk, wv, wza, wbg, wcg, wxi, wzb, wconv_ref,
                    q_ref, k_ref, v_ref, za_ref, ob_ref, tail_ref, u_scr, carry_scr, *, stride, tiles_per_seq):
    i = pl.program_id(0)
    j = pl.program_id(1)

    @pl.when(j == 0)
    def _():
        u_scr[...] = _modulate(x_ref[...], scale_ref[...], shift_ref[...]).astype(BF16)

    @pl.when(i % tiles_per_seq == 0)
    def _():
        carry_scr[j] = state_ref[...]

    u = u_scr[...]
    q_ref[...] = (_dot(u, wq[...]) * Q_SCALE).astype(q_ref.dtype)
    k_ref[...] = _dot(u, wk[...])
    v_ref[...] = _dot(u, wv[...])
    za_ref[...] = _dot(u, wza[...])
    ci = _dot(u, wcg[...]) * _dot(u, wxi[...])
    prev = carry_scr[j]
    wc = wconv_ref[...]
    conv = wc[2:3] * ci + wc[1:2] * _shift_rows(ci, prev, stride) + wc[0:1] * _shift_rows(ci, prev, 2 * stride)
    ob_ref[...] = (_dot(u, wbg[...]) * conv * _silu(_dot(u, wzb[...]))).astype(ob_ref.dtype)
    new_prev = _next_prev(ci, prev)
    carry_scr[j] = new_prev
    tail_ref[...] = new_prev


def _inproj0(x, shift, scale, state, w_in, w_conv, *, tm, tn, stride, tiles_per_seq, q_dtype):
    rows, d = x.shape
    c = w_conv.shape[1]
    p = state.shape[1]
    n_i, n_j = rows // tm, c // tn
    gm = shift.shape[-2]
    if shift.ndim == 3:
        mod_spec = pl.BlockSpec((None, gm, d), lambda i, j: (i // tiles_per_seq, 0, 0))
    else:
        mod_spec = pl.BlockSpec((gm, d), lambda i, j: (0, 0))

    def w_spec(g):
        return pl.BlockSpec((d, tn), lambda i, j: (0, g * n_j + j))

    act_spec = pl.BlockSpec((tm, tn), lambda i, j: (i, j))
    outs = pl.pallas_call(
        functools.partial(_inproj0_kernel, stride=stride, tiles_per_seq=tiles_per_seq),
        grid=(n_i, n_j),
        in_specs=[pl.BlockSpec((tm, d), lambda i, j: (i, 0)), mod_spec, mod_spec,
                  pl.BlockSpec((None, p, tn), lambda i, j: (i // tiles_per_seq, 0, j))]
                 + [w_spec(g) for g in range(8)]
                 + [pl.BlockSpec((w_conv.shape[0], tn), lambda i, j: (0, j))],
        out_specs=[act_spec] * 5 + [pl.BlockSpec((None, p, tn), lambda i, j: (i, 0, j))],
        out_shape=[jax.ShapeDtypeStruct((rows, c), q_dtype)] + [jax.ShapeDtypeStruct((rows, c), F32)] * 3
                  + [jax.ShapeDtypeStruct((rows, c), BF16), jax.ShapeDtypeStruct((n_i, p, c), F32)],
        scratch_shapes=[pltpu.VMEM((tm, d), BF16), pltpu.VMEM((n_j, p, tn), F32)],
        compiler_params=_cparams(2),
        name="inproj0",
    )(x, shift, scale, state, *([w_in] * 8), w_conv)
    return outs


def _sb_softplus(z2, mask):
    neg_abs = lax.bitcast_convert_type(lax.bitcast_convert_type(z2, jnp.uint32) | jnp.uint32(0x80000000), F32)
    sp = jnp.maximum(z2, 0.0) + jnp.log2(1.0 + jnp.exp2(neg_abs))
    return sp if mask is None else jnp.where(mask, sp, 0.0)


def _sb_exp(z2, incl, carry, mask):
    w = jnp.exp2(z2 - incl - carry)
    return (w if mask is None else jnp.where(mask, w, 0.0)).astype(BF16)


def _tri(t):
    r = lax.broadcasted_iota(jnp.int32, (t, t), 0)
    c = lax.broadcasted_iota(jnp.int32, (t, t), 1)
    return jnp.where(r >= c, 1.0, 0.0).astype(BF16)


def _attn_prompt_kernel(bias_ref, q_ref, k_ref, v_ref, za_ref, o_ref, *, t, heads):
    hg = pl.program_id(1)
    qi = pl.program_id(2)
    tri = _tri(t)
    r = lax.broadcasted_iota(jnp.int32, (t, t), 0)
    c = lax.broadcasted_iota(jnp.int32, (t, t), 1)
    lanes = [slice(g * HEAD_DIM, (g + 1) * HEAD_DIM) for g in range(heads)]
    qs = [q_ref[:, ln] for ln in lanes]
    biases = [bias_ref[0, hg * heads + g] * LOG2E for g in range(heads)]

    def block(kj, state, mask):
        start = pl.multiple_of(kj * t, t)
        zs = [_dot_nt(qs[g], k_ref[pl.ds(start, t), ln].astype(BF16)) + biases[g] for g, ln in enumerate(lanes)]
        sps = [_sb_softplus(z2, mask) for z2 in zs]
        incls = [_dot(sp.astype(BF16), tri) for sp in sps]
        ws = [_sb_exp(zs[g], incls[g], state[2 * g], mask) for g in range(heads)]
        out = []
        for g, ln in enumerate(lanes):
            vb = v_ref[pl.ds(start, t), ln].astype(BF16)
            out += [state[2 * g] + jnp.sum(sps[g], axis=-1, keepdims=True), state[2 * g + 1] + _dot(ws[g], vb)]
        return tuple(out)

    state = (jnp.zeros((t, 1), F32), jnp.zeros((t, HEAD_DIM), F32)) * heads
    state = block(qi, state, c < r)
    state = lax.fori_loop(0, qi, lambda s, st: block(qi - 1 - s, st, None), state)
    for g, ln in enumerate(lanes):
        o_ref[:, ln] = (state[2 * g + 1] * _silu(za_ref[:, ln])).astype(o_ref.dtype)


def _attn_prompt(bias, q, k, v, za, *, n_batch, seq, heads=4):
    rows, width = k.shape
    n_heads = width // HEAD_DIM
    t = SB_BLOCK
    nq = seq // t
    q_spec = pl.BlockSpec((t, heads * HEAD_DIM), lambda b, h, qi: (b * nq + qi, h))
    kv_spec = pl.BlockSpec((seq, heads * HEAD_DIM), lambda b, h, qi: (b, h))
    return pl.pallas_call(
        functools.partial(_attn_prompt_kernel, t=t, heads=heads),
        grid=(n_batch, n_heads // heads, nq),
        in_specs=[pl.BlockSpec(memory_space=pltpu.SMEM), q_spec, kv_spec, kv_spec, q_spec],
        out_specs=q_spec,
        out_shape=jax.ShapeDtypeStruct((rows, width), BF16),
        compiler_params=_cparams(3),
        name="attn_prompt",
    )(bias, q, k, v, za)


def _attn_sample_kernel(pt_ref, bias_ref, q_ref, kn_ref, vn_ref, za_ref, *rest, n_pages, page, n_heads):
    del pt_ref
    k_pages = rest[:n_pages]
    v_pages = rest[n_pages:2 * n_pages]
    o_ref, k_all, v_all = rest[2 * n_pages:]
    t_new, width = kn_ref.shape
    past = n_pages * page
    n_blk = n_pages + 1
    for p in range(n_pages):
        for h in range(n_heads):
            rows = pl.ds(h, page, stride=n_heads)
            dst = (slice(p * page, (p + 1) * page), slice(h * HEAD_DIM, (h + 1) * HEAD_DIM))
            k_all[dst] = k_pages[p][rows, :].astype(BF16)
            v_all[dst] = v_pages[p][rows, :].astype(BF16)
    pad = jnp.zeros((page - t_new, width), F32)
    k_all[past:past + page, :] = jnp.concatenate([kn_ref[...], pad], axis=0).astype(BF16)
    v_all[past:past + page, :] = jnp.concatenate([vn_ref[...], pad], axis=0).astype(BF16)

    m = n_heads * t_new
    row = lax.broadcasted_iota(jnp.int32, (m, width), 0)
    lane = lax.broadcasted_iota(jnp.int32, (m, width), 1)
    q_tiled = jnp.concatenate([q_ref[...]] * n_heads, axis=0)
    q_bd = jnp.where(_idiv(row, t_new) == _idiv(lane, HEAD_DIM), q_tiled, 0.0).astype(BF16)
    row1 = lax.broadcasted_iota(jnp.int32, (m, 1), 0)
    bias = jnp.zeros((m, 1), F32)
    for h in range(n_heads):
        bias = jnp.where(_idiv(row1, t_new) == h, bias_ref[0, h] * LOG2E, bias)
    z_all = _dot_nt(q_bd, k_all[...]) + bias

    tri = _tri(page)
    rb = lax.broadcasted_iota(jnp.int32, (m, page), 0)
    cb = lax.broadcasted_iota(jnp.int32, (m, page), 1)
    new_mask = cb < _imod(rb, t_new)
    carry = jnp.zeros((m, 1), F32)
    w_blocks = [None] * n_blk
    for j in range(n_blk - 1, -1, -1):
        z2 = z_all[:, j * page:(j + 1) * page]
        mask = new_mask if j == n_blk - 1 else None
        sp = _sb_softplus(z2, mask)
        w_blocks[j] = _sb_exp(z2, _dot(sp.astype(BF16), tri), carry, mask)
        carry = carry + jnp.sum(sp, axis=-1, keepdims=True)
    out = _dot(jnp.concatenate(w_blocks, axis=1), v_all[...])
    o = jnp.concatenate([out[h * t_new:(h + 1) * t_new, h * HEAD_DIM:(h + 1) * HEAD_DIM]
                         for h in range(n_heads)], axis=1)
    o_ref[...] = (o * _silu(za_ref[...])).astype(o_ref.dtype)


def _attn_sample(page_table, bias, q, k_new, v_new, za, cache_k, cache_v, *, n_phys):
    n_seq, t_new, width = q.shape
    n_pages = page_table.shape[1]
    n_heads = width // HEAD_DIM
    page = cache_k.shape[0] // (n_phys * n_heads)
    seq_spec = pl.BlockSpec((None, t_new, width), lambda n, pt: (n, 0, 0))

    def page_spec(p):
        return pl.BlockSpec((page * n_heads, HEAD_DIM), lambda n, pt: (pt[n, p], 0))

    grid_spec = pltpu.PrefetchScalarGridSpec(
        num_scalar_prefetch=1,
        grid=(n_seq,),
        in_specs=[pl.BlockSpec(memory_space=pltpu.SMEM), seq_spec, seq_spec, seq_spec, seq_spec]
                 + [page_spec(p) for p in range(n_pages)] * 2,
        out_specs=seq_spec,
        scratch_shapes=[pltpu.VMEM(((n_pages + 1) * page, width), BF16)] * 2,
    )
    return pl.pallas_call(
        functools.partial(_attn_sample_kernel, n_pages=n_pages, page=page, n_heads=n_heads),
        grid_spec=grid_spec,
        out_shape=jax.ShapeDtypeStruct((n_seq, t_new, width), F32),
        compiler_params=_cparams(1),
        name="attn_sample",
    )(page_table, bias, q, k_new, v_new, za, *([cache_k] * n_pages), *([cache_v] * n_pages))


def _outproj_ln_kernel(*refs, n_in, alpha):
    x_ref, gate_ref = refs[0], refs[1]
    acts = refs[2:2 + n_in]
    ws = refs[2 + n_in:2 + 2 * n_in]
    g_ref, b_ref, y_ref = refs[2 + 2 * n_in:]
    mix = _dot(acts[0][...].astype(BF16), ws[0][...])
    for a, w in zip(acts[1:], ws[1:]):
        mix = mix + _dot(a[...].astype(BF16), w[...])
    r = alpha * x_ref[...] + _gate_mix(mix, gate_ref[...])
    mu = jnp.mean(r, axis=-1, keepdims=True)
    rc = r - mu
    var = jnp.mean(rc * rc, axis=-1, keepdims=True)
    y_ref[...] = rc * lax.rsqrt(var + LN_EPS) * g_ref[...] + b_ref[...]


def _outproj_ln(x, gate, acts, w_out, ln_g, ln_b, *, tm, tiles_per_seq, alpha):
    rows, d = x.shape
    gm = gate.shape[-2]
    if gate.ndim == 3:
        mod_spec = pl.BlockSpec((None, gm, d), lambda i: (i // tiles_per_seq, 0, 0))
    else:
        mod_spec = pl.BlockSpec((gm, d), lambda i: (0, 0))
    n_in = len(acts)
    kw = w_out.shape[0] // n_in
    once = pl.Buffered(1)
    w_specs = [pl.BlockSpec((kw, d), functools.partial(lambda i, g: (g, 0), g=g), pipeline_mode=once)
               for g in range(n_in)]
    vec_spec = pl.BlockSpec((1, d), lambda i: (0, 0))
    return pl.pallas_call(
        functools.partial(_outproj_ln_kernel, n_in=n_in, alpha=alpha),
        grid=(rows // tm,),
        in_specs=[pl.BlockSpec((tm, d), lambda i: (i, 0)), mod_spec]
                 + [pl.BlockSpec((tm, kw), lambda i: (i, 0))] * n_in + w_specs + [vec_spec, vec_spec],
        out_specs=pl.BlockSpec((tm, d), lambda i: (i, 0)),
        out_shape=jax.ShapeDtypeStruct((rows, d), F32),
        compiler_params=_cparams(1),
        name="outproj_ln",
    )(x, gate, *acts, *([w_out] * n_in), ln_g.reshape(1, d), ln_b.reshape(1, d))


def _scan_rows(a_scr, b_scr, h_scr, h0):
    def body(g, h):
        r0 = pl.multiple_of(g * SUBLANES, SUBLANES)
        a8 = a_scr[pl.ds(r0, SUBLANES), :]
        b8 = b_scr[pl.ds(r0, SUBLANES), :]
        out = []
        for r in range(SUBLANES):
            h = a8[r:r + 1] * h + b8[r:r + 1]
            out.append(h)
        h_scr[pl.ds(r0, SUBLANES), :] = jnp.concatenate(out, axis=0)
        return h
    return lax.fori_loop(0, a_scr.shape[0] // SUBLANES, body, h0)


def _lru_kernel(x_ref, shift_ref, scale_ref, state_ref, h0_ref, wx_ref, wz_ref, wconv_ref, bconv_ref,
                wga_ref, wgx_ref, bga_ref, bgx_ref, lam_ref,
                o_ref, tail_ref, hlast_ref, prev_scr, hcar_scr, a_scr, b_scr, h_scr, *, stride, tiles_per_seq):
    i = pl.program_id(0)

    @pl.when(i % tiles_per_seq == 0)
    def _():
        prev_scr[...] = state_ref[...]
        hcar_scr[...] = h0_ref[...]

    u = _modulate(x_ref[...], scale_ref[...], shift_ref[...]).astype(BF16)
    xr = _dot(u, wx_ref[...])
    prev = prev_scr[...]
    wc = wconv_ref[...]
    xc = wc[3:4] * xr + bconv_ref[...]
    for kk in range(1, 4):
        xc = xc + wc[3 - kk:4 - kk] * _shift_rows(xr, prev, kk * stride)
    new_prev = _next_prev(xr, prev)
    prev_scr[...] = new_prev
    tail_ref[...] = new_prev

    xcb = xc.astype(BF16)
    n_blocks, blk, _ = wga_ref.shape
    ga = jnp.concatenate([_dot(xcb[:, n * blk:(n + 1) * blk], wga_ref[n]) for n in range(n_blocks)], axis=1)
    gx = jnp.concatenate([_dot(xcb[:, n * blk:(n + 1) * blk], wgx_ref[n]) for n in range(n_blocks)], axis=1)
    lam = lam_ref[...]
    neg_log_sig = jnp.maximum(-lam, 0.0) + jnp.log1p(jnp.exp(-jnp.abs(lam)))
    log_a = (-LRU_C) * jax.nn.sigmoid(ga + bga_ref[...]) * neg_log_sig
    a = jnp.exp(log_a)
    bt = jnp.sqrt(jnp.tanh(-log_a) * (1.0 + a * a)) * jax.nn.sigmoid(gx + bgx_ref[...]) * xc

    if stride == x_ref.shape[0]:
        h = a * hcar_scr[...] + bt
        hcar_scr[...] = h
    else:
        a_scr[...] = a
        b_scr[...] = bt
        hcar_scr[...] = _scan_rows(a_scr, b_scr, h_scr, hcar_scr[...])
        h = h_scr[...]
    hlast_ref[...] = hcar_scr[...]
    z = _dot(u, wz_ref[...])
    o_ref[...] = (h * _silu(z)).astype(o_ref.dtype)


def _lru(x, shift, scale, state, h0, w_in, w_conv, b_conv, w_ga, w_gx, b_ga, b_gx, lam, *, tm, stride,
         tiles_per_seq):
    rows, d = x.shape
    p = state.shape[1]
    hr = h0.shape[1]
    n_i = rows // tm
    gm = shift.shape[-2]
    if shift.ndim == 3:
        mod_spec = pl.BlockSpec((None, gm, d), lambda i: (i // tiles_per_seq, 0, 0))
    else:
        mod_spec = pl.BlockSpec((gm, d), lambda i: (0, 0))
    once = pl.Buffered(1)
    vec_spec = pl.BlockSpec((1, d), lambda i: (0, 0))
    gate_spec = pl.BlockSpec(w_ga.shape, lambda i: (0, 0, 0), pipeline_mode=once)
    scan_scr = (tm, d) if stride != tm else (SUBLANES, 128)
    return pl.pallas_call(
        functools.partial(_lru_kernel, stride=stride, tiles_per_seq=tiles_per_seq),
        grid=(n_i,),
        in_specs=[pl.BlockSpec((tm, d), lambda i: (i, 0)), mod_spec, mod_spec,
                  pl.BlockSpec((None, p, d), lambda i: (i // tiles_per_seq, 0, 0)),
                  pl.BlockSpec((None, hr, d), lambda i: (i // tiles_per_seq, 0, 0)),
                  pl.BlockSpec((d, d), lambda i: (0, 0), pipeline_mode=once),
                  pl.BlockSpec((d, d), lambda i: (0, 1), pipeline_mode=once),
                  pl.BlockSpec((w_conv.shape[0], d), lambda i: (0, 0)), vec_spec,
                  gate_spec, gate_spec, vec_spec, vec_spec, vec_spec],
        out_specs=[pl.BlockSpec((tm, d), lambda i: (i, 0)),
                   pl.BlockSpec((None, p, d), lambda i: (i, 0, 0)),
                   pl.BlockSpec((None, hr, d), lambda i: (i, 0, 0))],
        out_shape=[jax.ShapeDtypeStruct((rows, d), BF16), jax.ShapeDtypeStruct((n_i, p, d), F32),
                   jax.ShapeDtypeStruct((n_i, hr, d), F32)],
        scratch_shapes=[pltpu.VMEM((p, d), F32), pltpu.VMEM((hr, d), F32)]
                       + [pltpu.VMEM(scan_scr, F32)] * 3,
        compiler_params=_cparams(1),
        name="lru",
    )(x, shift, scale, state, h0, w_in, w_in, w_conv, b_conv.reshape(1, d), w_ga, w_gx,
      b_ga.reshape(1, d), b_gx.reshape(1, d), lam.reshape(1, d))


def kernel(x_prompt, x_sample, c_prompt, c_sample, cache_k, cache_v, state_conv_b, state_conv_c, state_h, page_table, we_ada, be_ada, we_in, we_sb_bias, we_conv, we_out, ge_ln, be_ln, wo_ada, bo_ada, wo_in, wo_conv, bo_conv, wo_gate_a, bo_gate_a, wo_gate_x, bo_gate_x, wo_lambda, wo_out, go_ln, bo_ln):
    nb, seq, d = x_prompt.shape
    ns, dec, _ = x_sample.shape
    assert we_ada.shape[0] == 1 and wo_ada.shape[0] == 1, "one even and one odd layer"
    n_heads = we_sb_bias.shape[1]
    width = n_heads * HEAD_DIM
    alpha = (2.0 * (we_ada.shape[0] + wo_ada.shape[0])) ** 0.25
    tm = 512

    def to_time_major(a):
        return jnp.swapaxes(a, 0, 1).reshape(a.shape[1] * ns, a.shape[2])

    def to_seq_major(a, t):
        return jnp.swapaxes(a.reshape(t, ns, a.shape[-1]), 0, 1)

    pad = (-(nb + ns)) % SUBLANES
    c_all = jnp.concatenate([c_prompt, c_sample, jnp.zeros((pad, d), F32)], axis=0)
    mods = []
    for w_ada, b_ada in ((we_ada[0], be_ada[0]), (wo_ada[0], bo_ada[0])):
        m = _ada(c_all, w_ada, b_ada)
        parts = [m[:, g * d:(g + 1) * d] for g in range(3)]
        mods.append(([v[:nb].reshape(nb, 1, d) for v in parts], [v[nb:nb + ns] for v in parts]))

    xp = x_prompt.reshape(nb * seq, d)
    xs = to_time_major(x_sample)
    bias = we_sb_bias[0].reshape(1, n_heads)

    (shift_p, scale_p, gate_p), (shift_s, scale_s, gate_s) = mods[0]
    w_in0 = we_in[0].astype(BF16)
    w_out0 = we_out[0].astype(BF16)
    conv_k = we_conv.shape[1]
    qp, kp, vp, zap, obp, tailp = _inproj0(
        xp, shift_p, scale_p, jnp.zeros((nb, SUBLANES, width), F32), w_in0, we_conv[0],
        tm=tm, tn=256, stride=1, tiles_per_seq=seq // tm, q_dtype=BF16)
    oap = _attn_prompt(bias, qp, kp, vp, zap, n_batch=nb, seq=seq)
    yp = _outproj_ln(xp, gate_p, [oap, obp], w_out0, ge_ln[0], be_ln[0], tm=tm, tiles_per_seq=seq // tm,
                     alpha=alpha)
    conv_b_prompt = tailp.reshape(nb, seq // tm, SUBLANES, width)[:, -1, SUBLANES - (conv_k - 1):][None]

    state_b = to_time_major(state_conv_b[0])[None]
    rows_s = ns * dec
    tiles_s = rows_s // tm
    qs, ks, vs, zas, obs, tails = _inproj0(
        xs, shift_s, scale_s, state_b, w_in0, we_conv[0],
        tm=tm, tn=256, stride=ns, tiles_per_seq=tiles_s, q_dtype=F32)
    ks_sm, vs_sm = to_seq_major(ks, dec), to_seq_major(vs, dec)
    n_phys = cache_k.shape[1]
    oas = _attn_sample(page_table, bias, to_seq_major(qs, dec), ks_sm, vs_sm, to_seq_major(zas, dec),
                       cache_k.reshape(-1, HEAD_DIM), cache_v.reshape(-1, HEAD_DIM), n_phys=n_phys)
    ys = _outproj_ln(xs, gate_s, [to_time_major(oas), obs], w_out0, ge_ln[0], be_ln[0], tm=tm,
                     tiles_per_seq=tiles_s, alpha=alpha)
    conv_b_sample = to_seq_major(tails[-1], conv_k - 1)[None]

    (shift_p, scale_p, gate_p), (shift_s, scale_s, gate_s) = mods[1]
    w_in1 = wo_in[0].astype(BF16)
    w_out1 = wo_out[0].astype(BF16)
    w_ga = wo_gate_a[0].astype(BF16)
    w_gx = wo_gate_x[0].astype(BF16)
    conv_k = wo_conv.shape[1]
    lru_w = (w_in1, wo_conv[0], bo_conv[0], w_ga, w_gx, bo_gate_a[0], bo_gate_x[0], wo_lambda[0])
    tm1 = 256
    op, tailp, hp = _lru(yp, shift_p, scale_p, jnp.zeros((nb, SUBLANES, d), F32), jnp.zeros((nb, 1, d), F32),
                         *lru_w, tm=tm1, stride=1, tiles_per_seq=seq // tm1)
    yp = _outproj_ln(yp, gate_p, [op], w_out1, go_ln[0], bo_ln[0], tm=tm, tiles_per_seq=seq // tm, alpha=alpha)
    conv_c_prompt = tailp.reshape(nb, seq // tm1, SUBLANES, d)[:, -1, SUBLANES - (conv_k - 1):][None]
    h_prompt = hp.reshape(nb, seq // tm1, d)[:, -1][None]

    state_c = to_time_major(state_conv_c[0])[None]
    os_, tails, hs = _lru(ys, shift_s, scale_s, state_c, state_h[0][None], *lru_w, tm=ns, stride=ns,
                          tiles_per_seq=dec)
    ys = _outproj_ln(ys, gate_s, [os_], w_out1, go_ln[0], bo_ln[0], tm=tm, tiles_per_seq=tiles_s, alpha=alpha)
    conv_c_sample = to_seq_major(tails[-1], conv_k - 1)[None]
    h_sample = hs[-1][None]

    y_prompt = yp.reshape(nb, seq, d)
    y_sample = to_seq_major(ys, dec)
    k_prompt = kp.reshape(1, nb, seq, n_heads, HEAD_DIM)
    v_prompt = vp.reshape(1, nb, seq, n_heads, HEAD_DIM)
    k_sample = ks_sm.reshape(1, ns, dec, n_heads, HEAD_DIM)
    v_sample = vs_sm.reshape(1, ns, dec, n_heads, HEAD_DIM)
    return (y_prompt, y_sample, k_prompt, v_prompt, conv_b_prompt, conv_c_prompt, h_prompt,
            k_sample, v_sample, conv_b_sample, conv_c_sample, h_sample)
```

```python
import functools
import math

import jax
import jax.numpy as jnp
from jax import lax
from jax.experimental import pallas as pl
from jax.experimental.pallas import tpu as pltpu

F32 = jnp.float32
BF16 = jnp.bfloat16

HEAD_DIM = 128
SB_BLOCK = 256
LRU_C = 8.0
LN_EPS = 1e-5
LOG2E = math.log2(math.e)
Q_SCALE = LOG2E / math.sqrt(HEAD_DIM)
SUBLANES = 8
VMEM_LIMIT = 56 * 1024 * 1024


def _cparams(n_axes):
    return pltpu.CompilerParams(dimension_semantics=("arbitrary",) * n_axes, vmem_limit_bytes=VMEM_LIMIT)


def _silu(x):
    return x * jax.nn.sigmoid(x)


def _idiv(x, n):
    assert n & (n - 1) == 0
    return x >> (n.bit_length() - 1)


def _imod(x, n):
    assert n & (n - 1) == 0
    return x & (n - 1)


def _dot(a, b):
    return jnp.dot(a, b, preferred_element_type=F32)


def _dot_nt(a, b):
    return lax.dot_general(a, b, (((1,), (1,)), ((), ())), preferred_element_type=F32)


def _modulate(x, scale, shift):
    gm = scale.shape[0]
    if gm == 1:
        return x * (1.0 + scale) + shift
    rows, d = x.shape
    x3 = x.reshape(rows // gm, gm, d)
    return (x3 * (1.0 + scale)[None] + shift[None]).reshape(rows, d)


def _gate_mix(mix, gate):
    gm = gate.shape[0]
    if gm == 1:
        return mix * (1.0 + gate)
    rows, d = mix.shape
    return (mix.reshape(rows // gm, gm, d) * (1.0 + gate)[None]).reshape(rows, d)


def _shift_rows(cur, prev, d):
    tm, p = cur.shape[0], prev.shape[0]
    if d % SUBLANES == 0:
        if d >= tm:
            return prev[p - d:p - d + tm]
        return jnp.concatenate([prev[p - d:], cur[:tm - d]], axis=0)
    assert d < SUBLANES <= p
    rolled = pltpu.roll(cur, d, 0)
    prev_rolled = pltpu.roll(prev[p - SUBLANES:], d, 0)
    row = lax.broadcasted_iota(jnp.int32, prev_rolled.shape, 0)
    head = jnp.where(row < d, prev_rolled, rolled[:SUBLANES])
    return jnp.concatenate([head, rolled[SUBLANES:]], axis=0)


def _next_prev(cur, prev):
    tm, p = cur.shape[0], prev.shape[0]
    if tm >= p:
        return cur[tm - p:]
    return jnp.concatenate([prev[tm:], cur], axis=0)


def _ada_kernel(c_ref, w_ref, b_ref, o_ref):
    s = _silu(c_ref[...]).astype(BF16)
    o_ref[...] = _dot(s, w_ref[...].astype(BF16)) + b_ref[...]


def _ada(c_all, w_ada, b_ada, tn=1024):
    rows, d = c_all.shape
    n = w_ada.shape[1]
    return pl.pallas_call(
        _ada_kernel,
        grid=(n // tn,),
        in_specs=[pl.BlockSpec((rows, d), lambda j: (0, 0)),
                  pl.BlockSpec((d, tn), lambda j: (0, j)),
                  pl.BlockSpec((1, tn), lambda j: (0, j))],
        out_specs=pl.BlockSpec((rows, tn), lambda j: (0, j)),
        out_shape=jax.ShapeDtypeStruct((rows, n), F32),
        compiler_params=_cparams(1),
        name="ada",
    )(c_all, w_ada, b_ada.reshape(1, n))


def _inproj0_kernel(x_ref, shift_ref, scale_ref, state_ref, wq, wk, wv, wza, wbg, wcg, wxi, wzb, wconv_ref,
                    q_ref, k_ref, v_ref, za_ref, ob_ref, tail_ref, u_scr, carry_scr, *, stride, tiles_per_seq):
    i = pl.program_id(0)
    j = pl.program_id(1)

    @pl.when(j == 0)
    def _():
        u_scr[...] = _modulate(x_ref[...], scale_ref[...], shift_ref[...]).astype(BF16)

    @pl.when(i % tiles_per_seq == 0)
    def _():
        carry_scr[j] = state_ref[...]

    u = u_scr[...]
    q_ref[...] = (_dot(u, wq[...]) * Q_SCALE).astype(q_ref.dtype)
    k_ref[...] = _dot(u, wk[...])
    v_ref[...] = _dot(u, wv[...])
    za_ref[...] = _dot(u, wza[...])
    ci = _dot(u, wcg[...]) * _dot(u, wxi[...])
    prev = carry_scr[j]
    wc = wconv_ref[...]
    conv = wc[2:3] * ci + wc[1:2] * _shift_rows(ci, prev, stride) + wc[0:1] * _shift_rows(ci, prev, 2 * stride)
    ob_ref[...] = (_dot(u, wbg[...]) * conv * _silu(_dot(u, wzb[...]))).astype(ob_ref.dtype)
    new_prev = _next_prev(ci, prev)
    carry_scr[j] = new_prev
    tail_ref[...] = new_prev


def _inproj0(x, shift, scale, state, w_in, w_conv, *, tm, tn, stride, tiles_per_seq, q_dtype):
    rows, d = x.shape
    c = w_conv.shape[1]
    p = state.shape[1]
    n_i, n_j = rows // tm, c // tn
    gm = shift.shape[-2]
    if shift.ndim == 3:
        mod_spec = pl.BlockSpec((None, gm, d), lambda i, j: (i // tiles_per_seq, 0, 0))
    else:
        mod_spec = pl.BlockSpec((gm, d), lambda i, j: (0, 0))

    def w_spec(g):
        return pl.BlockSpec((d, tn), lambda i, j: (0, g * n_j + j))

    act_spec = pl.BlockSpec((tm, tn), lambda i, j: (i, j))
    outs = pl.pallas_call(
        functools.partial(_inproj0_kernel, stride=stride, tiles_per_seq=tiles_per_seq),
        grid=(n_i, n_j),
        in_specs=[pl.BlockSpec((tm, d), lambda i, j: (i, 0)), mod_spec, mod_spec,
                  pl.BlockSpec((None, p, tn), lambda i, j: (i // tiles_per_seq, 0, j))]
                 + [w_spec(g) for g in range(8)]
                 + [pl.BlockSpec((w_conv.shape[0], tn), lambda i, j: (0, j))],
        out_specs=[act_spec] * 5 + [pl.BlockSpec((None, p, tn), lambda i, j: (i, 0, j))],
        out_shape=[jax.ShapeDtypeStruct((rows, c), q_dtype)] + [jax.ShapeDtypeStruct((rows, c), F32)] * 3
                  + [jax.ShapeDtypeStruct((rows, c), BF16), jax.ShapeDtypeStruct((n_i, p, c), F32)],
        scratch_shapes=[pltpu.VMEM((tm, d), BF16), pltpu.VMEM((n_j, p, tn), F32)],
        compiler_params=_cparams(2),
        name="inproj0",
    )(x, shift, scale, state, *([w_in] * 8), w_conv)
    return outs


def _sb_softplus(z2, mask):
    neg_abs = lax.bitcast_convert_type(lax.bitcast_convert_type(z2, jnp.uint32) | jnp.uint32(0x80000000), F32)
    sp = jnp.maximum(z2, 0.0) + jnp.log2(1.0 + jnp.exp2(neg_abs))
    return sp if mask is None else jnp.where(mask, sp, 0.0)


def _sb_exp(z2, incl, carry, mask):
    w = jnp.exp2(z2 - incl - carry)
    return (w if mask is None else jnp.where(mask, w, 0.0)).astype(BF16)


def _tri(t):
    r = lax.broadcasted_iota(jnp.int32, (t, t), 0)
    c = lax.broadcasted_iota(jnp.int32, (t, t), 1)
    return jnp.where(r >= c, 1.0, 0.0).astype(BF16)


def _attn_prompt_kernel(bias_ref, q_ref, k_ref, v_ref, za_ref, o_ref, *, t, heads):
    hg = pl.program_id(1)
    qi = pl.program_id(2)
    tri = _tri(t)
    r = lax.broadcasted_iota(jnp.int32, (t, t), 0)
    c = lax.broadcasted_iota(jnp.int32, (t, t), 1)
    lanes = [slice(g * HEAD_DIM, (g + 1) * HEAD_DIM) for g in range(heads)]
    qs = [q_ref[:, ln] for ln in lanes]
    biases = [bias_ref[0, hg * heads + g] * LOG2E for g in range(heads)]

    def block(kj, state, mask):
        start = pl.multiple_of(kj * t, t)
        zs = [_dot_nt(qs[g], k_ref[pl.ds(start, t), ln].astype(BF16)) + biases[g] for g, ln in enumerate(lanes)]
        sps = [_sb_softplus(z2, mask) for z2 in zs]
        incls = [_dot(sp.astype(BF16), tri) for sp in sps]
        ws = [_sb_exp(zs[g], incls[g], state[2 * g], mask) for g in range(heads)]
        out = []
        for g, ln in enumerate(lanes):
            vb = v_ref[pl.ds(start, t), ln].astype(BF16)
            out += [state[2 * g] + jnp.sum(sps[g], axis=-1, keepdims=True), state[2 * g + 1] + _dot(ws[g], vb)]
        return tuple(out)

    state = (jnp.zeros((t, 1), F32), jnp.zeros((t, HEAD_DIM), F32)) * heads
    state = block(qi, state, c < r)
    state = lax.fori_loop(0, qi, lambda s, st: block(qi - 1 - s, st, None), state)
    for g, ln in enumerate(lanes):
        o_ref[:, ln] = (state[2 * g + 1] * _silu(za_ref[:, ln])).astype(o_ref.dtype)


def _attn_prompt(bias, q, k, v, za, *, n_batch, seq, heads=8):
    rows, width = k.shape
    n_heads = width // HEAD_DIM
    t = SB_BLOCK
    nq = seq // t
    q_spec = pl.BlockSpec((t, heads * HEAD_DIM), lambda b, h, qi: (b * nq + qi, h))
    kv_spec = pl.BlockSpec((seq, heads * HEAD_DIM), lambda b, h, qi: (b, h))
    return pl.pallas_call(
        functools.partial(_attn_prompt_kernel, t=t, heads=heads),
        grid=(n_batch, n_heads // heads, nq),
        in_specs=[pl.BlockSpec(memory_space=pltpu.SMEM), q_spec, kv_spec, kv_spec, q_spec],
        out_specs=q_spec,
        out_shape=jax.ShapeDtypeStruct((rows, width), BF16),
        compiler_params=_cparams(3),
        name="attn_prompt",
    )(bias, q, k, v, za)


def _attn_sample_kernel(pt_ref, bias_ref, q_ref, kn_ref, vn_ref, za_ref, *rest, n_pages, page, n_heads):
    del pt_ref
    k_pages = rest[:n_pages]
    v_pages = rest[n_pages:2 * n_pages]
    o_ref, k_all, v_all = rest[2 * n_pages:]
    t_new, width = kn_ref.shape
    past = n_pages * page
    n_blk = n_pages + 1
    chunk = 2 * SUBLANES
    for p in range(n_pages):
        for c in range(page // chunk):
            for h in range(n_heads):
                rows = pl.ds(c * chunk * n_heads + h, chunk, stride=n_heads)
                dst = (slice(p * page + c * chunk, p * page + (c + 1) * chunk),
                       slice(h * HEAD_DIM, (h + 1) * HEAD_DIM))
                k_all[dst] = k_pages[p][rows, :].astype(BF16)
                v_all[dst] = v_pages[p][rows, :].astype(BF16)
    pad = jnp.zeros((page - t_new, width), F32)
    k_all[past:past + page, :] = jnp.concatenate([kn_ref[...], pad], axis=0).astype(BF16)
    v_all[past:past + page, :] = jnp.concatenate([vn_ref[...], pad], axis=0).astype(BF16)

    m = n_heads * t_new
    row = lax.broadcasted_iota(jnp.int32, (m, width), 0)
    lane = lax.broadcasted_iota(jnp.int32, (m, width), 1)
    q_tiled = jnp.concatenate([q_ref[...]] * n_heads, axis=0)
    q_bd = jnp.where(_idiv(row, t_new) == _idiv(lane, HEAD_DIM), q_tiled, 0.0).astype(BF16)
    row1 = lax.broadcasted_iota(jnp.int32, (m, 1), 0)
    bias = jnp.zeros((m, 1), F32)
    for h in range(n_heads):
        bias = jnp.where(_idiv(row1, t_new) == h, bias_ref[0, h] * LOG2E, bias)
    z_all = _dot_nt(q_bd, k_all[...]) + bias

    tri = _tri(page)
    rb = lax.broadcasted_iota(jnp.int32, (m, page), 0)
    cb = lax.broadcasted_iota(jnp.int32, (m, page), 1)
    new_mask = cb < _imod(rb, t_new)
    carry = jnp.zeros((m, 1), F32)
    w_blocks = [None] * n_blk
    for j in range(n_blk - 1, -1, -1):
        z2 = z_all[:, j * page:(j + 1) * page]
        mask = new_mask if j == n_blk - 1 else None
        sp = _sb_softplus(z2, mask)
        w_blocks[j] = _sb_exp(z2, _dot(sp.astype(BF16), tri), carry, mask)
        carry = carry + jnp.sum(sp, axis=-1, keepdims=True)
    out = _dot(jnp.concatenate(w_blocks, axis=1), v_all[...])
    o = jnp.concatenate([out[h * t_new:(h + 1) * t_new, h * HEAD_DIM:(h + 1) * HEAD_DIM]
                         for h in range(n_heads)], axis=1)
    o_ref[...] = (o * _silu(za_ref[...])).astype(o_ref.dtype)


def _attn_sample(page_table, bias, q, k_new, v_new, za, cache_k, cache_v, *, n_phys):
    n_seq, t_new, width = q.shape
    n_pages = page_table.shape[1]
    n_heads = width // HEAD_DIM
    page = cache_k.shape[0] // (n_phys * n_heads)
    seq_spec = pl.BlockSpec((None, t_new, width), lambda n, pt: (n, 0, 0))

    def page_spec(p):
        return pl.BlockSpec((page * n_heads, HEAD_DIM), lambda n, pt: (pt[n, p], 0))

    grid_spec = pltpu.PrefetchScalarGridSpec(
        num_scalar_prefetch=1,
        grid=(n_seq,),
        in_specs=[pl.BlockSpec(memory_space=pltpu.SMEM), seq_spec, seq_spec, seq_spec, seq_spec]
                 + [page_spec(p) for p in range(n_pages)] * 2,
        out_specs=seq_spec,
        scratch_shapes=[pltpu.VMEM(((n_pages + 1) * page, width), BF16)] * 2,
    )
    return pl.pallas_call(
        functools.partial(_attn_sample_kernel, n_pages=n_pages, page=page, n_heads=n_heads),
        grid_spec=grid_spec,
        out_shape=jax.ShapeDtypeStruct((n_seq, t_new, width), F32),
        compiler_params=_cparams(1),
        name="attn_sample",
    )(page_table, bias, q, k_new, v_new, za, *([cache_k] * n_pages), *([cache_v] * n_pages))


def _outproj_ln_kernel(*refs, n_in, alpha):
    x_ref, gate_ref = refs[0], refs[1]
    acts = refs[2:2 + n_in]
    ws = refs[2 + n_in:2 + 2 * n_in]
    g_ref, b_ref, y_ref = refs[2 + 2 * n_in:]
    mix = _dot(acts[0][...].astype(BF16), ws[0][...])
    for a, w in zip(acts[1:], ws[1:]):
        mix = mix + _dot(a[...].astype(BF16), w[...])
    r = alpha * x_ref[...] + _gate_mix(mix, gate_ref[...])
    mu = jnp.mean(r, axis=-1, keepdims=True)
    rc = r - mu
    var = jnp.mean(rc * rc, axis=-1, keepdims=True)
    y_ref[...] = rc * lax.rsqrt(var + LN_EPS) * g_ref[...] + b_ref[...]


def _outproj_ln(x, gate, acts, w_out, ln_g, ln_b, *, tm, tiles_per_seq, alpha):
    rows, d = x.shape
    gm = gate.shape[-2]
    if gate.ndim == 3:
        mod_spec = pl.BlockSpec((None, gm, d), lambda i: (i // tiles_per_seq, 0, 0))
    else:
        mod_spec = pl.BlockSpec((gm, d), lambda i: (0, 0))
    n_in = len(acts)
    kw = w_out.shape[0] // n_in
    once = pl.Buffered(1)
    w_specs = [pl.BlockSpec((kw, d), functools.partial(lambda i, g: (g, 0), g=g), pipeline_mode=once)
               for g in range(n_in)]
    vec_spec = pl.BlockSpec((1, d), lambda i: (0, 0))
    return pl.pallas_call(
        functools.partial(_outproj_ln_kernel, n_in=n_in, alpha=alpha),
        grid=(rows // tm,),
        in_specs=[pl.BlockSpec((tm, d), lambda i: (i, 0)), mod_spec]
                 + [pl.BlockSpec((tm, kw), lambda i: (i, 0))] * n_in + w_specs + [vec_spec, vec_spec],
        out_specs=pl.BlockSpec((tm, d), lambda i: (i, 0)),
        out_shape=jax.ShapeDtypeStruct((rows, d), F32),
        compiler_params=_cparams(1),
        name="outproj_ln",
    )(x, gate, *acts, *([w_out] * n_in), ln_g.reshape(1, d), ln_b.reshape(1, d))


def _scan_rows(a_scr, b_scr, h_scr, h0):
    def body(g, h):
        r0 = pl.multiple_of(g * SUBLANES, SUBLANES)
        a8 = a_scr[pl.ds(r0, SUBLANES), :]
        b8 = b_scr[pl.ds(r0, SUBLANES), :]
        out = []
        for r in range(SUBLANES):
            h = a8[r:r + 1] * h + b8[r:r + 1]
            out.append(h)
        h_scr[pl.ds(r0, SUBLANES), :] = jnp.concatenate(out, axis=0)
        return h
    return lax.fori_loop(0, a_scr.shape[0] // SUBLANES, body, h0)


def _lru_kernel(x_ref, shift_ref, scale_ref, state_ref, h0_ref, wx_ref, wz_ref, wconv_ref, bconv_ref,
                wga_ref, wgx_ref, bga_ref, bgx_ref, lam_ref,
                o_ref, tail_ref, hlast_ref, prev_scr, hcar_scr, a_scr, b_scr, h_scr, *, stride, tiles_per_seq):
    i = pl.program_id(0)

    @pl.when(i % tiles_per_seq == 0)
    def _():
        prev_scr[...] = state_ref[...]
        hcar_scr[...] = h0_ref[...]

    u = _modulate(x_ref[...], scale_ref[...], shift_ref[...]).astype(BF16)
    xr = _dot(u, wx_ref[...])
    prev = prev_scr[...]
    wc = wconv_ref[...]
    xc = wc[3:4] * xr + bconv_ref[...]
    for kk in range(1, 4):
        xc = xc + wc[3 - kk:4 - kk] * _shift_rows(xr, prev, kk * stride)
    new_prev = _next_prev(xr, prev)
    prev_scr[...] = new_prev
    tail_ref[...] = new_prev

    xcb = xc.astype(BF16)
    n_blocks, blk, _ = wga_ref.shape
    ga = jnp.concatenate([_dot(xcb[:, n * blk:(n + 1) * blk], wga_ref[n]) for n in range(n_blocks)], axis=1)
    gx = jnp.concatenate([_dot(xcb[:, n * blk:(n + 1) * blk], wgx_ref[n]) for n in range(n_blocks)], axis=1)
    lam = lam_ref[...]
    neg_log_sig = jnp.maximum(-lam, 0.0) + jnp.log1p(jnp.exp(-jnp.abs(lam)))
    log_a = (-LRU_C) * jax.nn.sigmoid(ga + bga_ref[...]) * neg_log_sig
    a = jnp.exp(log_a)
    bt = jnp.sqrt(jnp.tanh(-log_a) * (1.0 + a * a)) * jax.nn.sigmoid(gx + bgx_ref[...]) * xc

    if stride == x_ref.shape[0]:
        h = a * hcar_scr[...] + bt
        hcar_scr[...] = h
    else:
        a_scr[...] = a
        b_scr[...] = bt
        hcar_scr[...] = _scan_rows(a_scr, b_scr, h_scr, hcar_scr[...])
        h = h_scr[...]
    hlast_ref[...] = hcar_scr[...]
    z = _dot(u, wz_ref[...])
    o_ref[...] = (h * _silu(z)).astype(o_ref.dtype)


def _lru(x, shift, scale, state, h0, w_in, w_conv, b_conv, w_ga, w_gx, b_ga, b_gx, lam, *, tm, stride,
         tiles_per_seq):
    rows, d = x.shape
    p = state.shape[1]
    hr = h0.shape[1]
    n_i = rows // tm
    gm = shift.shape[-2]
    if shift.ndim == 3:
        mod_spec = pl.BlockSpec((None, gm, d), lambda i: (i // tiles_per_seq, 0, 0))
    else:
        mod_spec = pl.BlockSpec((gm, d), lambda i: (0, 0))
    once = pl.Buffered(1)
    vec_spec = pl.BlockSpec((1, d), lambda i: (0, 0))
    gate_spec = pl.BlockSpec(w_ga.shape, lambda i: (0, 0, 0), pipeline_mode=once)
    scan_scr = (tm, d) if stride != tm else (SUBLANES, 128)
    return pl.pallas_call(
        functools.partial(_lru_kernel, stride=stride, tiles_per_seq=tiles_per_seq),
        grid=(n_i,),
        in_specs=[pl.BlockSpec((tm, d), lambda i: (i, 0)), mod_spec, mod_spec,
                  pl.BlockSpec((None, p, d), lambda i: (i // tiles_per_seq, 0, 0)),
                  pl.BlockSpec((None, hr, d), lambda i: (i // tiles_per_seq, 0, 0)),
                  pl.BlockSpec((d, d), lambda i: (0, 0), pipeline_mode=once),
                  pl.BlockSpec((d, d), lambda i: (0, 1), pipeline_mode=once),
                  pl.BlockSpec((w_conv.shape[0], d), lambda i: (0, 0)), vec_spec,
                  gate_spec, gate_spec, vec_spec, vec_spec, vec_spec],
        out_specs=[pl.BlockSpec((tm, d), lambda i: (i, 0)),
                   pl.BlockSpec((None, p, d), lambda i: (i, 0, 0)),
                   pl.BlockSpec((None, hr, d), lambda i: (i, 0, 0))],
        out_shape=[jax.ShapeDtypeStruct((rows, d), BF16), jax.ShapeDtypeStruct((n_i, p, d), F32),
                   jax.ShapeDtypeStruct((n_i, hr, d), F32)],
        scratch_shapes=[pltpu.VMEM((p, d), F32), pltpu.VMEM((hr, d), F32)]
                       + [pltpu.VMEM(scan_scr, F32)] * 3,
        compiler_params=_cparams(1),
        name="lru",
    )(x, shift, scale, state, h0, w_in, w_in, w_conv, b_conv.reshape(1, d), w_ga, w_gx,
      b_ga.reshape(1, d), b_gx.reshape(1, d), lam.reshape(1, d))


def kernel(x_prompt, x_sample, c_prompt, c_sample, cache_k, cache_v, state_conv_b, state_conv_c, state_h, page_table, we_ada, be_ada, we_in, we_sb_bias, we_conv, we_out, ge_ln, be_ln, wo_ada, bo_ada, wo_in, wo_conv, bo_conv, wo_gate_a, bo_gate_a, wo_gate_x, bo_gate_x, wo_lambda, wo_out, go_ln, bo_ln):
    nb, seq, d = x_prompt.shape
    ns, dec, _ = x_sample.shape
    assert we_ada.shape[0] == 1 and wo_ada.shape[0] == 1, "one even and one odd layer"
    n_heads = we_sb_bias.shape[1]
    width = n_heads * HEAD_DIM
    alpha = (2.0 * (we_ada.shape[0] + wo_ada.shape[0])) ** 0.25
    tm = 512
    tm_in0 = 1024

    def to_time_major(a):
        return jnp.swapaxes(a, 0, 1).reshape(a.shape[1] * ns, a.shape[2])

    def to_seq_major(a, t):
        return jnp.swapaxes(a.reshape(t, ns, a.shape[-1]), 0, 1)

    pad = (-(nb + ns)) % SUBLANES
    c_all = jnp.concatenate([c_prompt, c_sample, jnp.zeros((pad, d), F32)], axis=0)
    mods = []
    for w_ada, b_ada in ((we_ada[0], be_ada[0]), (wo_ada[0], bo_ada[0])):
        m = _ada(c_all, w_ada, b_ada)
        parts = [m[:, g * d:(g + 1) * d] for g in range(3)]
        mods.append(([v[:nb].reshape(nb, 1, d) for v in parts], [v[nb:nb + ns] for v in parts]))

    xp = x_prompt.reshape(nb * seq, d)
    xs = to_time_major(x_sample)
    bias = we_sb_bias[0].reshape(1, n_heads)

    (shift_p, scale_p, gate_p), (shift_s, scale_s, gate_s) = mods[0]
    w_in0 = we_in[0].astype(BF16)
    w_out0 = we_out[0].astype(BF16)
    conv_k = we_conv.shape[1]
    qp, kp, vp, zap, obp, tailp = _inproj0(
        xp, shift_p, scale_p, jnp.zeros((nb, SUBLANES, width), F32), w_in0, we_conv[0],
        tm=tm_in0, tn=256, stride=1, tiles_per_seq=seq // tm_in0, q_dtype=BF16)
    oap = _attn_prompt(bias, qp, kp, vp, zap, n_batch=nb, seq=seq)
    yp = _outproj_ln(xp, gate_p, [oap, obp], w_out0, ge_ln[0], be_ln[0], tm=tm, tiles_per_seq=seq // tm,
                     alpha=alpha)
    conv_b_prompt = tailp.reshape(nb, seq // tm_in0, SUBLANES, width)[:, -1, SUBLANES - (conv_k - 1):][None]

    state_b = to_time_major(state_conv_b[0])[None]
    rows_s = ns * dec
    tiles_s = rows_s // tm
    qs, ks, vs, zas, obs, tails = _inproj0(
        xs, shift_s, scale_s, state_b, w_in0, we_conv[0],
        tm=tm, tn=256, stride=ns, tiles_per_seq=tiles_s, q_dtype=F32)
    ks_sm, vs_sm = to_seq_major(ks, dec), to_seq_major(vs, dec)
    n_phys = cache_k.shape[1]
    oas = _attn_sample(page_table, bias, to_seq_major(qs, dec), ks_sm, vs_sm, to_seq_major(zas, dec),
                       cache_k.reshape(-1, HEAD_DIM), cache_v.reshape(-1, HEAD_DIM), n_phys=n_phys)
    ys = _outproj_ln(xs, gate_s, [to_time_major(oas), obs], w_out0, ge_ln[0], be_ln[0], tm=tm,
                     tiles_per_seq=tiles_s, alpha=alpha)
    conv_b_sample = to_seq_major(tails[-1], conv_k - 1)[None]

    (shift_p, scale_p, gate_p), (shift_s, scale_s, gate_s) = mods[1]
    w_in1 = wo_in[0].astype(BF16)
    w_out1 = wo_out[0].astype(BF16)
    w_ga = wo_gate_a[0].astype(BF16)
    w_gx = wo_gate_x[0].astype(BF16)
    conv_k = wo_conv.shape[1]
    lru_w = (w_in1, wo_conv[0], bo_conv[0], w_ga, w_gx, bo_gate_a[0], bo_gate_x[0], wo_lambda[0])
    tm1 = 256
    op, tailp, hp = _lru(yp, shift_p, scale_p, jnp.zeros((nb, SUBLANES, d), F32), jnp.zeros((nb, 1, d), F32),
                         *lru_w, tm=tm1, stride=1, tiles_per_seq=seq // tm1)
    yp = _outproj_ln(yp, gate_p, [op], w_out1, go_ln[0], bo_ln[0], tm=tm, tiles_per_seq=seq // tm, alpha=alpha)
    conv_c_prompt = tailp.reshape(nb, seq // tm1, SUBLANES, d)[:, -1, SUBLANES - (conv_k - 1):][None]
    h_prompt = hp.reshape(nb, seq // tm1, d)[:, -1][None]

    state_c = to_time_major(state_conv_c[0])[None]
    os_, tails, hs = _lru(ys, shift_s, scale_s, state_c, state_h[0][None], *lru_w, tm=ns, stride=ns,
                          tiles_per_seq=dec)
    ys = _outproj_ln(ys, gate_s, [os_], w_out1, go_ln[0], bo_ln[0], tm=tm, tiles_per_seq=tiles_s, alpha=alpha)
    conv_c_sample = to_seq_major(tails[-1], conv_k - 1)[None]
    h_sample = hs[-1][None]

    y_prompt = yp.reshape(nb, seq, d)
    y_sample = to_seq_major(ys, dec)
    k_prompt = kp.reshape(1, nb, seq, n_heads, HEAD_DIM)
    v_prompt = vp.reshape(1, nb, seq, n_heads, HEAD_DIM)
    k_sample = ks_sm.reshape(1, ns, dec, n_heads, HEAD_DIM)
    v_sample = vs_sm.reshape(1, ns, dec, n_heads, HEAD_DIM)
    return (y_prompt, y_sample, k_prompt, v_prompt, conv_b_prompt, conv_c_prompt, h_prompt,
            k_sample, v_sample, conv_b_sample, conv_c_sample, h_sample)
```

```python
import functools
import math

import jax
import jax.numpy as jnp
from jax import lax
from jax.experimental import pallas as pl
from jax.experimental.pallas import tpu as pltpu

F32 = jnp.float32
BF16 = jnp.bfloat16

HEAD_DIM = 128
SB_BLOCK = 256
LRU_C = 8.0
LN_EPS = 1e-5
LOG2E = math.log2(math.e)
Q_SCALE = LOG2E / math.sqrt(HEAD_DIM)
SUBLANES = 8
VMEM_LIMIT = 56 * 1024 * 1024


def _cparams(n_axes):
    return pltpu.CompilerParams(dimension_semantics=("arbitrary",) * n_axes, vmem_limit_bytes=VMEM_LIMIT)


def _silu(x):
    return x * jax.nn.sigmoid(x)


def _idiv(x, n):
    assert n & (n - 1) == 0
    return x >> (n.bit_length() - 1)


def _imod(x, n):
    assert n & (n - 1) == 0
    return x & (n - 1)


def _dot(a, b):
    return jnp.dot(a, b, preferred_element_type=F32)


def _dot_nt(a, b):
    return lax.dot_general(a, b, (((1,), (1,)), ((), ())), preferred_element_type=F32)


def _modulate(x, scale, shift):
    gm = scale.shape[0]
    if gm == 1:
        return x * (1.0 + scale) + shift
    rows, d = x.shape
    x3 = x.reshape(rows // gm, gm, d)
    return (x3 * (1.0 + scale)[None] + shift[None]).reshape(rows, d)


def _gate_mix(mix, gate):
    gm = gate.shape[0]
    if gm == 1:
        return mix * (1.0 + gate)
    rows, d = mix.shape
    return (mix.reshape(rows // gm, gm, d) * (1.0 + gate)[None]).reshape(rows, d)


def _shift_rows(cur, prev, d):
    tm, p = cur.shape[0], prev.shape[0]
    if d % SUBLANES == 0:
        if d >= tm:
            return prev[p - d:p - d + tm]
        return jnp.concatenate([prev[p - d:], cur[:tm - d]], axis=0)
    assert d < SUBLANES <= p
    rolled = pltpu.roll(cur, d, 0)
    prev_rolled = pltpu.roll(prev[p - SUBLANES:], d, 0)
    row = lax.broadcasted_iota(jnp.int32, prev_rolled.shape, 0)
    head = jnp.where(row < d, prev_rolled, rolled[:SUBLANES])
    return jnp.concatenate([head, rolled[SUBLANES:]], axis=0)


def _next_prev(cur, prev):
    tm, p = cur.shape[0], prev.shape[0]
    if tm >= p:
        return cur[tm - p:]
    return jnp.concatenate([prev[tm:], cur], axis=0)


def _ada_kernel(c_ref, w_ref, b_ref, o_ref):
    s = _silu(c_ref[...]).astype(BF16)
    o_ref[...] = _dot(s, w_ref[...].astype(BF16)) + b_ref[...]


def _ada(c_all, w_ada, b_ada, tn=1024):
    rows, d = c_all.shape
    n = w_ada.shape[1]
    return pl.pallas_call(
        _ada_kernel,
        grid=(n // tn,),
        in_specs=[pl.BlockSpec((rows, d), lambda j: (0, 0)),
                  pl.BlockSpec((d, tn), lambda j: (0, j)),
                  pl.BlockSpec((1, tn), lambda j: (0, j))],
        out_specs=pl.BlockSpec((rows, tn), lambda j: (0, j)),
        out_shape=jax.ShapeDtypeStruct((rows, n), F32),
        compiler_params=_cparams(1),
        name="ada",
    )(c_all, w_ada, b_ada.reshape(1, n))


def _inproj0_kernel(x_ref, shift_ref, scale_ref, state_ref, wq, wk, wv, wza, wbg, wcg, wxi, wzb, wconv_ref,
                    q_ref, k_ref, v_ref, za_ref, ob_ref, tail_ref, u_scr, carry_scr, *, stride, tiles_per_seq):
    i = pl.program_id(0)
    j = pl.program_id(1)

    @pl.when(j == 0)
    def _():
        u_scr[...] = _modulate(x_ref[...], scale_ref[...], shift_ref[...]).astype(BF16)

    @pl.when(i % tiles_per_seq == 0)
    def _():
        carry_scr[j] = state_ref[...]

    u = u_scr[...]
    q_ref[...] = (_dot(u, wq[...]) * Q_SCALE).astype(q_ref.dtype)
    k_ref[...] = _dot(u, wk[...])
    v_ref[...] = _dot(u, wv[...])
    za_ref[...] = _dot(u, wza[...])
    ci = _dot(u, wcg[...]) * _dot(u, wxi[...])
    prev = carry_scr[j]
    wc = wconv_ref[...]
    conv = wc[2:3] * ci + wc[1:2] * _shift_rows(ci, prev, stride) + wc[0:1] * _shift_rows(ci, prev, 2 * stride)
    ob_ref[...] = (_dot(u, wbg[...]) * conv * _silu(_dot(u, wzb[...]))).astype(ob_ref.dtype)
    new_prev = _next_prev(ci, prev)
    carry_scr[j] = new_prev
    tail_ref[...] = new_prev


def _inproj0(x, shift, scale, state, w_in, w_conv, *, tm, tn, stride, tiles_per_seq, q_dtype):
    rows, d = x.shape
    c = w_conv.shape[1]
    p = state.shape[1]
    n_i, n_j = rows // tm, c // tn
    gm = shift.shape[-2]
    if shift.ndim == 3:
        mod_spec = pl.BlockSpec((None, gm, d), lambda i, j: (i // tiles_per_seq, 0, 0))
    else:
        mod_spec = pl.BlockSpec((gm, d), lambda i, j: (0, 0))

    def w_spec(g):
        return pl.BlockSpec((d, tn), lambda i, j: (0, g * n_j + j))

    act_spec = pl.BlockSpec((tm, tn), lambda i, j: (i, j))
    outs = pl.pallas_call(
        functools.partial(_inproj0_kernel, stride=stride, tiles_per_seq=tiles_per_seq),
        grid=(n_i, n_j),
        in_specs=[pl.BlockSpec((tm, d), lambda i, j: (i, 0)), mod_spec, mod_spec,
                  pl.BlockSpec((None, p, tn), lambda i, j: (i // tiles_per_seq, 0, j))]
                 + [w_spec(g) for g in range(8)]
                 + [pl.BlockSpec((w_conv.shape[0], tn), lambda i, j: (0, j))],
        out_specs=[act_spec] * 5 + [pl.BlockSpec((None, p, tn), lambda i, j: (i, 0, j))],
        out_shape=[jax.ShapeDtypeStruct((rows, c), q_dtype)] + [jax.ShapeDtypeStruct((rows, c), F32)] * 3
                  + [jax.ShapeDtypeStruct((rows, c), BF16), jax.ShapeDtypeStruct((n_i, p, c), F32)],
        scratch_shapes=[pltpu.VMEM((tm, d), BF16), pltpu.VMEM((n_j, p, tn), F32)],
        compiler_params=_cparams(2),
        name="inproj0",
    )(x, shift, scale, state, *([w_in] * 8), w_conv)
    return outs


def _sb_softplus(z2, mask):
    neg_abs = lax.bitcast_convert_type(lax.bitcast_convert_type(z2, jnp.uint32) | jnp.uint32(0x80000000), F32)
    sp = jnp.maximum(z2, 0.0) + jnp.log2(1.0 + jnp.exp2(neg_abs))
    return sp if mask is None else jnp.where(mask, sp, 0.0)


def _sb_exp(z2, incl, carry, mask):
    w = jnp.exp2(z2 - incl - carry)
    return (w if mask is None else jnp.where(mask, w, 0.0)).astype(BF16)


def _tri(t):
    r = lax.broadcasted_iota(jnp.int32, (t, t), 0)
    c = lax.broadcasted_iota(jnp.int32, (t, t), 1)
    return jnp.where(r >= c, 1.0, 0.0).astype(BF16)


def _attn_prompt_kernel(bias_ref, q_ref, k_ref, v_ref, za_ref, o_ref, *, t, heads):
    hg = pl.program_id(1)
    qi = pl.program_id(2)
    tri = _tri(t)
    r = lax.broadcasted_iota(jnp.int32, (t, t), 0)
    c = lax.broadcasted_iota(jnp.int32, (t, t), 1)
    lanes = [slice(g * HEAD_DIM, (g + 1) * HEAD_DIM) for g in range(heads)]
    qs = [q_ref[:, ln] for ln in lanes]
    biases = [bias_ref[0, hg * heads + g] * LOG2E for g in range(heads)]

    def block(kj, state, mask):
        start = pl.multiple_of(kj * t, t)
        zs = [_dot_nt(qs[g], k_ref[pl.ds(start, t), ln].astype(BF16)) + biases[g] for g, ln in enumerate(lanes)]
        sps = [_sb_softplus(z2, mask) for z2 in zs]
        incls = [_dot(sp.astype(BF16), tri) for sp in sps]
        ws = [_sb_exp(zs[g], incls[g], state[2 * g], mask) for g in range(heads)]
        out = []
        for g, ln in enumerate(lanes):
            vb = v_ref[pl.ds(start, t), ln].astype(BF16)
            out += [state[2 * g] + jnp.sum(sps[g], axis=-1, keepdims=True), state[2 * g + 1] + _dot(ws[g], vb)]
        return tuple(out)

    state = (jnp.zeros((t, 1), F32), jnp.zeros((t, HEAD_DIM), F32)) * heads
    state = block(qi, state, c < r)
    state = lax.fori_loop(0, qi, lambda s, st: block(qi - 1 - s, st, None), state)
    for g, ln in enumerate(lanes):
        o_ref[:, ln] = (state[2 * g + 1] * _silu(za_ref[:, ln])).astype(o_ref.dtype)


def _attn_prompt(bias, q, k, v, za, *, n_batch, seq, heads=8):
    rows, width = k.shape
    n_heads = width // HEAD_DIM
    t = SB_BLOCK
    nq = seq // t
    q_spec = pl.BlockSpec((t, heads * HEAD_DIM), lambda b, h, qi: (b * nq + qi, h))
    kv_spec = pl.BlockSpec((seq, heads * HEAD_DIM), lambda b, h, qi: (b, h))
    return pl.pallas_call(
        functools.partial(_attn_prompt_kernel, t=t, heads=heads),
        grid=(n_batch, n_heads // heads, nq),
        in_specs=[pl.BlockSpec(memory_space=pltpu.SMEM), q_spec, kv_spec, kv_spec, q_spec],
        out_specs=q_spec,
        out_shape=jax.ShapeDtypeStruct((rows, width), BF16),
        compiler_params=_cparams(3),
        name="attn_prompt",
    )(bias, q, k, v, za)


def _attn_sample_kernel(pt_ref, bias_ref, q_ref, kn_ref, vn_ref, za_ref, *rest, n_pages, page, n_heads):
    del pt_ref
    k_pages = rest[:n_pages]
    v_pages = rest[n_pages:2 * n_pages]
    o_ref, k_all, v_all = rest[2 * n_pages:]
    t_new, width = kn_ref.shape
    past = n_pages * page
    n_blk = n_pages + 1
    chunk = 2 * SUBLANES
    for p in range(n_pages):
        for c in range(page // chunk):
            for h in range(n_heads):
                rows = pl.ds(c * chunk * n_heads + h, chunk, stride=n_heads)
                dst = (slice(p * page + c * chunk, p * page + (c + 1) * chunk),
                       slice(h * HEAD_DIM, (h + 1) * HEAD_DIM))
                k_all[dst] = k_pages[p][rows, :].astype(BF16)
                v_all[dst] = v_pages[p][rows, :].astype(BF16)
    pad = jnp.zeros((page - t_new, width), F32)
    k_all[past:past + page, :] = jnp.concatenate([kn_ref[...], pad], axis=0).astype(BF16)
    v_all[past:past + page, :] = jnp.concatenate([vn_ref[...], pad], axis=0).astype(BF16)

    m = n_heads * t_new
    row = lax.broadcasted_iota(jnp.int32, (m, width), 0)
    lane = lax.broadcasted_iota(jnp.int32, (m, width), 1)
    q_tiled = jnp.concatenate([q_ref[...]] * n_heads, axis=0)
    q_bd = jnp.where(_idiv(row, t_new) == _idiv(lane, HEAD_DIM), q_tiled, 0.0).astype(BF16)
    row1 = lax.broadcasted_iota(jnp.int32, (m, 1), 0)
    bias = jnp.zeros((m, 1), F32)
    for h in range(n_heads):
        bias = jnp.where(_idiv(row1, t_new) == h, bias_ref[0, h] * LOG2E, bias)
    z_all = _dot_nt(q_bd, k_all[...]) + bias

    tri = _tri(page)
    rb = lax.broadcasted_iota(jnp.int32, (m, page), 0)
    cb = lax.broadcasted_iota(jnp.int32, (m, page), 1)
    new_mask = cb < _imod(rb, t_new)
    carry = jnp.zeros((m, 1), F32)
    w_blocks = [None] * n_blk
    for j in range(n_blk - 1, -1, -1):
        z2 = z_all[:, j * page:(j + 1) * page]
        mask = new_mask if j == n_blk - 1 else None
        sp = _sb_softplus(z2, mask)
        w_blocks[j] = _sb_exp(z2, _dot(sp.astype(BF16), tri), carry, mask)
        carry = carry + jnp.sum(sp, axis=-1, keepdims=True)
    out = _dot(jnp.concatenate(w_blocks, axis=1), v_all[...])
    o = jnp.concatenate([out[h * t_new:(h + 1) * t_new, h * HEAD_DIM:(h + 1) * HEAD_DIM]
                         for h in range(n_heads)], axis=1)
    o_ref[...] = (o * _silu(za_ref[...])).astype(o_ref.dtype)


def _attn_sample(page_table, bias, q, k_new, v_new, za, cache_k, cache_v, *, n_phys):
    n_seq, t_new, width = q.shape
    n_pages = page_table.shape[1]
    n_heads = width // HEAD_DIM
    page = cache_k.shape[0] // (n_phys * n_heads)
    seq_spec = pl.BlockSpec((None, t_new, width), lambda n, pt: (n, 0, 0))

    def page_spec(p):
        return pl.BlockSpec((page * n_heads, HEAD_DIM), lambda n, pt: (pt[n, p], 0))

    grid_spec = pltpu.PrefetchScalarGridSpec(
        num_scalar_prefetch=1,
        grid=(n_seq,),
        in_specs=[pl.BlockSpec(memory_space=pltpu.SMEM), seq_spec, seq_spec, seq_spec, seq_spec]
                 + [page_spec(p) for p in range(n_pages)] * 2,
        out_specs=seq_spec,
        scratch_shapes=[pltpu.VMEM(((n_pages + 1) * page, width), BF16)] * 2,
    )
    return pl.pallas_call(
        functools.partial(_attn_sample_kernel, n_pages=n_pages, page=page, n_heads=n_heads),
        grid_spec=grid_spec,
        out_shape=jax.ShapeDtypeStruct((n_seq, t_new, width), F32),
        compiler_params=_cparams(1),
        name="attn_sample",
    )(page_table, bias, q, k_new, v_new, za, *([cache_k] * n_pages), *([cache_v] * n_pages))


def _outproj_ln_kernel(*refs, n_in, alpha):
    x_ref, gate_ref = refs[0], refs[1]
    acts = refs[2:2 + n_in]
    ws = refs[2 + n_in:2 + 2 * n_in]
    g_ref, b_ref, y_ref = refs[2 + 2 * n_in:]
    mix = _dot(acts[0][...].astype(BF16), ws[0][...])
    for a, w in zip(acts[1:], ws[1:]):
        mix = mix + _dot(a[...].astype(BF16), w[...])
    r = alpha * x_ref[...] + _gate_mix(mix, gate_ref[...])
    mu = jnp.mean(r, axis=-1, keepdims=True)
    rc = r - mu
    var = jnp.mean(rc * rc, axis=-1, keepdims=True)
    y_ref[...] = rc * lax.rsqrt(var + LN_EPS) * g_ref[...] + b_ref[...]


def _outproj_ln(x, gate, acts, w_out, ln_g, ln_b, *, tm, tiles_per_seq, alpha):
    rows, d = x.shape
    gm = gate.shape[-2]
    if gate.ndim == 3:
        mod_spec = pl.BlockSpec((None, gm, d), lambda i: (i // tiles_per_seq, 0, 0))
    else:
        mod_spec = pl.BlockSpec((gm, d), lambda i: (0, 0))
    n_in = len(acts)
    kw = w_out.shape[0] // n_in
    once = pl.Buffered(1)
    w_specs = [pl.BlockSpec((kw, d), functools.partial(lambda i, g: (g, 0), g=g), pipeline_mode=once)
               for g in range(n_in)]
    vec_spec = pl.BlockSpec((1, d), lambda i: (0, 0))
    return pl.pallas_call(
        functools.partial(_outproj_ln_kernel, n_in=n_in, alpha=alpha),
        grid=(rows // tm,),
        in_specs=[pl.BlockSpec((tm, d), lambda i: (i, 0)), mod_spec]
                 + [pl.BlockSpec((tm, kw), lambda i: (i, 0))] * n_in + w_specs + [vec_spec, vec_spec],
        out_specs=pl.BlockSpec((tm, d), lambda i: (i, 0)),
        out_shape=jax.ShapeDtypeStruct((rows, d), F32),
        compiler_params=_cparams(1),
        name="outproj_ln",
    )(x, gate, *acts, *([w_out] * n_in), ln_g.reshape(1, d), ln_b.reshape(1, d))


def _scan_rows(a_scr, b_scr, h_scr, h0):
    h = h0
    for r0 in range(0, a_scr.shape[0], SUBLANES):
        a8 = a_scr[r0:r0 + SUBLANES, :]
        b8 = b_scr[r0:r0 + SUBLANES, :]
        out = []
        for r in range(SUBLANES):
            h = a8[r:r + 1] * h + b8[r:r + 1]
            out.append(h)
        h_scr[r0:r0 + SUBLANES, :] = jnp.concatenate(out, axis=0)
    return h


def _lru_kernel(x_ref, shift_ref, scale_ref, state_ref, h0_ref, wx_ref, wz_ref, wconv_ref, bconv_ref,
                wga_ref, wgx_ref, bga_ref, bgx_ref, lam_ref,
                o_ref, tail_ref, hlast_ref, prev_scr, hcar_scr, a_scr, b_scr, h_scr, *, stride, tiles_per_seq):
    i = pl.program_id(0)

    @pl.when(i % tiles_per_seq == 0)
    def _():
        prev_scr[...] = state_ref[...]
        hcar_scr[...] = h0_ref[...]

    u = _modulate(x_ref[...], scale_ref[...], shift_ref[...]).astype(BF16)
    xr = _dot(u, wx_ref[...])
    prev = prev_scr[...]
    wc = wconv_ref[...]
    xc = wc[3:4] * xr + bconv_ref[...]
    for kk in range(1, 4):
        xc = xc + wc[3 - kk:4 - kk] * _shift_rows(xr, prev, kk * stride)
    new_prev = _next_prev(xr, prev)
    prev_scr[...] = new_prev
    tail_ref[...] = new_prev

    xcb = xc.astype(BF16)
    n_blocks, blk, _ = wga_ref.shape
    ga = jnp.concatenate([_dot(xcb[:, n * blk:(n + 1) * blk], wga_ref[n]) for n in range(n_blocks)], axis=1)
    gx = jnp.concatenate([_dot(xcb[:, n * blk:(n + 1) * blk], wgx_ref[n]) for n in range(n_blocks)], axis=1)
    lam = lam_ref[...]
    neg_log_sig = jnp.maximum(-lam, 0.0) + jnp.log1p(jnp.exp(-jnp.abs(lam)))
    log_a = (-LRU_C) * jax.nn.sigmoid(ga + bga_ref[...]) * neg_log_sig
    a = jnp.exp(log_a)
    bt = jnp.sqrt(jnp.tanh(-log_a) * (1.0 + a * a)) * jax.nn.sigmoid(gx + bgx_ref[...]) * xc

    if stride == x_ref.shape[0]:
        h = a * hcar_scr[...] + bt
        hcar_scr[...] = h
    else:
        a_scr[...] = a
        b_scr[...] = bt
        hcar_scr[...] = _scan_rows(a_scr, b_scr, h_scr, hcar_scr[...])
        h = h_scr[...]
    hlast_ref[...] = hcar_scr[...]
    z = _dot(u, wz_ref[...])
    o_ref[...] = (h * _silu(z)).astype(o_ref.dtype)


def _lru(x, shift, scale, state, h0, w_in, w_conv, b_conv, w_ga, w_gx, b_ga, b_gx, lam, *, tm, stride,
         tiles_per_seq):
    rows, d = x.shape
    p = state.shape[1]
    hr = h0.shape[1]
    n_i = rows // tm
    gm = shift.shape[-2]
    if shift.ndim == 3:
        mod_spec = pl.BlockSpec((None, gm, d), lambda i: (i // tiles_per_seq, 0, 0))
    else:
        mod_spec = pl.BlockSpec((gm, d), lambda i: (0, 0))
    once = pl.Buffered(1)
    vec_spec = pl.BlockSpec((1, d), lambda i: (0, 0))
    gate_spec = pl.BlockSpec(w_ga.shape, lambda i: (0, 0, 0), pipeline_mode=once)
    scan_scr = (tm, d) if stride != tm else (SUBLANES, 128)
    return pl.pallas_call(
        functools.partial(_lru_kernel, stride=stride, tiles_per_seq=tiles_per_seq),
        grid=(n_i,),
        in_specs=[pl.BlockSpec((tm, d), lambda i: (i, 0)), mod_spec, mod_spec,
                  pl.BlockSpec((None, p, d), lambda i: (i // tiles_per_seq, 0, 0)),
                  pl.BlockSpec((None, hr, d), lambda i: (i // tiles_per_seq, 0, 0)),
                  pl.BlockSpec((d, d), lambda i: (0, 0), pipeline_mode=once),
                  pl.BlockSpec((d, d), lambda i: (0, 1), pipeline_mode=once),
                  pl.BlockSpec((w_conv.shape[0], d), lambda i: (0, 0)), vec_spec,
                  gate_spec, gate_spec, vec_spec, vec_spec, vec_spec],
        out_specs=[pl.BlockSpec((tm, d), lambda i: (i, 0)),
                   pl.BlockSpec((None, p, d), lambda i: (i, 0, 0)),
                   pl.BlockSpec((None, hr, d), lambda i: (i, 0, 0))],
        out_shape=[jax.ShapeDtypeStruct((rows, d), BF16), jax.ShapeDtypeStruct((n_i, p, d), F32),
                   jax.ShapeDtypeStruct((n_i, hr, d), F32)],
        scratch_shapes=[pltpu.VMEM((p, d), F32), pltpu.VMEM((hr, d), F32)]
                       + [pltpu.VMEM(scan_scr, F32)] * 3,
        compiler_params=_cparams(1),
        name="lru",
    )(x, shift, scale, state, h0, w_in, w_in, w_conv, b_conv.reshape(1, d), w_ga, w_gx,
      b_ga.reshape(1, d), b_gx.reshape(1, d), lam.reshape(1, d))


def kernel(x_prompt, x_sample, c_prompt, c_sample, cache_k, cache_v, state_conv_b, state_conv_c, state_h, page_table, we_ada, be_ada, we_in, we_sb_bias, we_conv, we_out, ge_ln, be_ln, wo_ada, bo_ada, wo_in, wo_conv, bo_conv, wo_gate_a, bo_gate_a, wo_gate_x, bo_gate_x, wo_lambda, wo_out, go_ln, bo_ln):
    nb, seq, d = x_prompt.shape
    ns, dec, _ = x_sample.shape
    assert we_ada.shape[0] == 1 and wo_ada.shape[0] == 1, "one even and one odd layer"
    n_heads = we_sb_bias.shape[1]
    width = n_heads * HEAD_DIM
    alpha = (2.0 * (we_ada.shape[0] + wo_ada.shape[0])) ** 0.25
    tm = 512
    tm_in0 = 1024

    def to_time_major(a):
        return jnp.swapaxes(a, 0, 1).reshape(a.shape[1] * ns, a.shape[2])

    def to_seq_major(a, t):
        return jnp.swapaxes(a.reshape(t, ns, a.shape[-1]), 0, 1)

    pad = (-(nb + ns)) % SUBLANES
    c_all = jnp.concatenate([c_prompt, c_sample, jnp.zeros((pad, d), F32)], axis=0)
    mods = []
    for w_ada, b_ada in ((we_ada[0], be_ada[0]), (wo_ada[0], bo_ada[0])):
        m = _ada(c_all, w_ada, b_ada)
        parts = [m[:, g * d:(g + 1) * d] for g in range(3)]
        mods.append(([v[:nb].reshape(nb, 1, d) for v in parts], [v[nb:nb + ns] for v in parts]))

    xp = x_prompt.reshape(nb * seq, d)
    xs = to_time_major(x_sample)
    bias = we_sb_bias[0].reshape(1, n_heads)

    (shift_p, scale_p, gate_p), (shift_s, scale_s, gate_s) = mods[0]
    w_in0 = we_in[0].astype(BF16)
    w_out0 = we_out[0].astype(BF16)
    conv_k = we_conv.shape[1]
    qp, kp, vp, zap, obp, tailp = _inproj0(
        xp, shift_p, scale_p, jnp.zeros((nb, SUBLANES, width), F32), w_in0, we_conv[0],
        tm=tm_in0, tn=256, stride=1, tiles_per_seq=seq // tm_in0, q_dtype=BF16)
    oap = _attn_prompt(bias, qp, kp, vp, zap, n_batch=nb, seq=seq)
    yp = _outproj_ln(xp, gate_p, [oap, obp], w_out0, ge_ln[0], be_ln[0], tm=tm, tiles_per_seq=seq // tm,
                     alpha=alpha)
    conv_b_prompt = tailp.reshape(nb, seq // tm_in0, SUBLANES, width)[:, -1, SUBLANES - (conv_k - 1):][None]

    state_b = to_time_major(state_conv_b[0])[None]
    rows_s = ns * dec
    tiles_s = rows_s // tm
    qs, ks, vs, zas, obs, tails = _inproj0(
        xs, shift_s, scale_s, state_b, w_in0, we_conv[0],
        tm=tm, tn=256, stride=ns, tiles_per_seq=tiles_s, q_dtype=F32)
    ks_sm, vs_sm = to_seq_major(ks, dec), to_seq_major(vs, dec)
    n_phys = cache_k.shape[1]
    oas = _attn_sample(page_table, bias, to_seq_major(qs, dec), ks_sm, vs_sm, to_seq_major(zas, dec),
                       cache_k.reshape(-1, HEAD_DIM), cache_v.reshape(-1, HEAD_DIM), n_phys=n_phys)
    ys = _outproj_ln(xs, gate_s, [to_time_major(oas), obs], w_out0, ge_ln[0], be_ln[0], tm=tm,
                     tiles_per_seq=tiles_s, alpha=alpha)
    conv_b_sample = to_seq_major(tails[-1], conv_k - 1)[None]

    (shift_p, scale_p, gate_p), (shift_s, scale_s, gate_s) = mods[1]
    w_in1 = wo_in[0].astype(BF16)
    w_out1 = wo_out[0].astype(BF16)
    w_ga = wo_gate_a[0].astype(BF16)
    w_gx = wo_gate_x[0].astype(BF16)
    conv_k = wo_conv.shape[1]
    lru_w = (w_in1, wo_conv[0], bo_conv[0], w_ga, w_gx, bo_gate_a[0], bo_gate_x[0], wo_lambda[0])
    tm1 = 256
    op, tailp, hp = _lru(yp, shift_p, scale_p, jnp.zeros((nb, SUBLANES, d), F32), jnp.zeros((nb, 1, d), F32),
                         *lru_w, tm=tm1, stride=1, tiles_per_seq=seq // tm1)
    yp = _outproj_ln(yp, gate_p, [op], w_out1, go_ln[0], bo_ln[0], tm=tm, tiles_per_seq=seq // tm, alpha=alpha)
    conv_c_prompt = tailp.reshape(nb, seq // tm1, SUBLANES, d)[:, -1, SUBLANES - (conv_k - 1):][None]
    h_prompt = hp.reshape(nb, seq // tm1, d)[:, -1][None]

    state_c = to_time_major(state_conv_c[0])[None]
    os_, tails, hs = _lru(ys, shift_s, scale_s, state_c, state_h[0][None], *lru_w, tm=ns, stride=ns,
                          tiles_per_seq=dec)
    ys = _outproj_ln(ys, gate_s, [os_], w_out1, go_ln[0], bo_ln[0], tm=tm, tiles_per_seq=tiles_s, alpha=alpha)
    conv_c_sample = to_seq_major(tails[-1], conv_k - 1)[None]
    h_sample = hs[-1][None]

    y_prompt = yp.reshape(nb, seq, d)
    y_sample = to_seq_major(ys, dec)
    k_prompt = kp.reshape(1, nb, seq, n_heads, HEAD_DIM)
    v_prompt = vp.reshape(1, nb, seq, n_heads, HEAD_DIM)
    k_sample = ks_sm.reshape(1, ns, dec, n_heads, HEAD_DIM)
    v_sample = vs_sm.reshape(1, ns, dec, n_heads, HEAD_DIM)
    return (y_prompt, y_sample, k_prompt, v_prompt, conv_b_prompt, conv_c_prompt, h_prompt,
            k_sample, v_sample, conv_b_sample, conv_c_sample, h_sample)
```

```python
import functools
import math

import jax
import jax.numpy as jnp
from jax import lax
from jax.experimental import pallas as pl
from jax.experimental.pallas import tpu as pltpu

F32 = jnp.float32
BF16 = jnp.bfloat16

HEAD_DIM = 128
SB_BLOCK = 256
LRU_C = 8.0
LN_EPS = 1e-5
LOG2E = math.log2(math.e)
Q_SCALE = LOG2E / math.sqrt(HEAD_DIM)
SUBLANES = 8
VMEM_LIMIT = 56 * 1024 * 1024


def _cparams(n_axes):
    return pltpu.CompilerParams(dimension_semantics=("arbitrary",) * n_axes, vmem_limit_bytes=VMEM_LIMIT)


def _silu(x):
    return x * jax.nn.sigmoid(x)


def _idiv(x, n):
    assert n & (n - 1) == 0
    return x >> (n.bit_length() - 1)


def _imod(x, n):
    assert n & (n - 1) == 0
    return x & (n - 1)


def _dot(a, b):
    return jnp.dot(a, b, preferred_element_type=F32)


def _dot_nt(a, b):
    return lax.dot_general(a, b, (((1,), (1,)), ((), ())), preferred_element_type=F32)


def _modulate(x, scale, shift):
    gm = scale.shape[0]
    if gm == 1:
        return x * (1.0 + scale) + shift
    rows, d = x.shape
    x3 = x.reshape(rows // gm, gm, d)
    return (x3 * (1.0 + scale)[None] + shift[None]).reshape(rows, d)


def _gate_mix(mix, gate):
    gm = gate.shape[0]
    if gm == 1:
        return mix * (1.0 + gate)
    rows, d = mix.shape
    return (mix.reshape(rows // gm, gm, d) * (1.0 + gate)[None]).reshape(rows, d)


def _shift_rows(cur, prev, d):
    tm, p = cur.shape[0], prev.shape[0]
    if d % SUBLANES == 0:
        if d >= tm:
            return prev[p - d:p - d + tm]
        return jnp.concatenate([prev[p - d:], cur[:tm - d]], axis=0)
    assert d < SUBLANES <= p
    rolled = pltpu.roll(cur, d, 0)
    prev_rolled = pltpu.roll(prev[p - SUBLANES:], d, 0)
    row = lax.broadcasted_iota(jnp.int32, prev_rolled.shape, 0)
    head = jnp.where(row < d, prev_rolled, rolled[:SUBLANES])
    return jnp.concatenate([head, rolled[SUBLANES:]], axis=0)


def _next_prev(cur, prev):
    tm, p = cur.shape[0], prev.shape[0]
    if tm >= p:
        return cur[tm - p:]
    return jnp.concatenate([prev[tm:], cur], axis=0)


def _ada_kernel(c_ref, w_ref, b_ref, o_ref):
    s = _silu(c_ref[...]).astype(BF16)
    o_ref[...] = _dot(s, w_ref[...].astype(BF16)) + b_ref[...]


def _ada(c_all, w_ada, b_ada, tn=1024):
    rows, d = c_all.shape
    n = w_ada.shape[1]
    return pl.pallas_call(
        _ada_kernel,
        grid=(n // tn,),
        in_specs=[pl.BlockSpec((rows, d), lambda j: (0, 0)),
                  pl.BlockSpec((d, tn), lambda j: (0, j)),
                  pl.BlockSpec((1, tn), lambda j: (0, j))],
        out_specs=pl.BlockSpec((rows, tn), lambda j: (0, j)),
        out_shape=jax.ShapeDtypeStruct((rows, n), F32),
        compiler_params=_cparams(1),
        name="ada",
    )(c_all, w_ada, b_ada.reshape(1, n))


def _inproj0_kernel(x_ref, shift_ref, scale_ref, state_ref, wq, wk, wv, wza, wbg, wcg, wxi, wzb, wconv_ref,
                    q_ref, k_ref, v_ref, za_ref, ob_ref, tail_ref, u_scr, carry_scr, *, stride, tiles_per_seq):
    i = pl.program_id(0)
    j = pl.program_id(1)

    @pl.when(j == 0)
    def _():
        u_scr[...] = _modulate(x_ref[...], scale_ref[...], shift_ref[...]).astype(BF16)

    @pl.when(i % tiles_per_seq == 0)
    def _():
        carry_scr[j] = state_ref[...]

    u = u_scr[...]
    q_ref[...] = (_dot(u, wq[...]) * Q_SCALE).astype(q_ref.dtype)
    k_ref[...] = _dot(u, wk[...])
    v_ref[...] = _dot(u, wv[...])
    za_ref[...] = _dot(u, wza[...])
    ci = _dot(u, wcg[...]) * _dot(u, wxi[...])
    prev = carry_scr[j]
    wc = wconv_ref[...]
    conv = wc[2:3] * ci + wc[1:2] * _shift_rows(ci, prev, stride) + wc[0:1] * _shift_rows(ci, prev, 2 * stride)
    ob_ref[...] = (_dot(u, wbg[...]) * conv * _silu(_dot(u, wzb[...]))).astype(ob_ref.dtype)
    new_prev = _next_prev(ci, prev)
    carry_scr[j] = new_prev
    tail_ref[...] = new_prev


def _inproj0(x, shift, scale, state, w_in, w_conv, *, tm, tn, stride, tiles_per_seq, q_dtype):
    rows, d = x.shape
    c = w_conv.shape[1]
    p = state.shape[1]
    n_i, n_j = rows // tm, c // tn
    gm = shift.shape[-2]
    if shift.ndim == 3:
        mod_spec = pl.BlockSpec((None, gm, d), lambda i, j: (i // tiles_per_seq, 0, 0))
    else:
        mod_spec = pl.BlockSpec((gm, d), lambda i, j: (0, 0))

    def w_spec(g):
        return pl.BlockSpec((d, tn), lambda i, j: (0, g * n_j + j))

    act_spec = pl.BlockSpec((tm, tn), lambda i, j: (i, j))
    outs = pl.pallas_call(
        functools.partial(_inproj0_kernel, stride=stride, tiles_per_seq=tiles_per_seq),
        grid=(n_i, n_j),
        in_specs=[pl.BlockSpec((tm, d), lambda i, j: (i, 0)), mod_spec, mod_spec,
                  pl.BlockSpec((None, p, tn), lambda i, j: (i // tiles_per_seq, 0, j))]
                 + [w_spec(g) for g in range(8)]
                 + [pl.BlockSpec((w_conv.shape[0], tn), lambda i, j: (0, j))],
        out_specs=[act_spec] * 5 + [pl.BlockSpec((None, p, tn), lambda i, j: (i, 0, j))],
        out_shape=[jax.ShapeDtypeStruct((rows, c), q_dtype)] + [jax.ShapeDtypeStruct((rows, c), F32)] * 3
                  + [jax.ShapeDtypeStruct((rows, c), BF16), jax.ShapeDtypeStruct((n_i, p, c), F32)],
        scratch_shapes=[pltpu.VMEM((tm, d), BF16), pltpu.VMEM((n_j, p, tn), F32)],
        compiler_params=_cparams(2),
        name="inproj0",
    )(x, shift, scale, state, *([w_in] * 8), w_conv)
    return outs


def _sb_softplus(z2, mask):
    neg_abs = lax.bitcast_convert_type(lax.bitcast_convert_type(z2, jnp.uint32) | jnp.uint32(0x80000000), F32)
    sp = jnp.maximum(z2, 0.0) + jnp.log2(1.0 + jnp.exp2(neg_abs))
    return sp if mask is None else jnp.where(mask, sp, 0.0)


def _sb_exp(z2, incl, carry, mask):
    w = jnp.exp2(z2 - incl - carry)
    return (w if mask is None else jnp.where(mask, w, 0.0)).astype(BF16)


def _tri(t):
    r = lax.broadcasted_iota(jnp.int32, (t, t), 0)
    c = lax.broadcasted_iota(jnp.int32, (t, t), 1)
    return jnp.where(r >= c, 1.0, 0.0).astype(BF16)


def _attn_prompt_kernel(bias_ref, q_ref, k_ref, v_ref, za_ref, o_ref, *, t, heads):
    hg = pl.program_id(1)
    qi = pl.program_id(2)
    tri = _tri(t)
    r = lax.broadcasted_iota(jnp.int32, (t, t), 0)
    c = lax.broadcasted_iota(jnp.int32, (t, t), 1)
    lanes = [slice(g * HEAD_DIM, (g + 1) * HEAD_DIM) for g in range(heads)]
    qs = [q_ref[:, ln] for ln in lanes]
    biases = [bias_ref[0, hg * heads + g] * LOG2E for g in range(heads)]

    def block(kj, state, mask):
        start = pl.multiple_of(kj * t, t)
        zs = [_dot_nt(qs[g], k_ref[pl.ds(start, t), ln].astype(BF16)) + biases[g] for g, ln in enumerate(lanes)]
        sps = [_sb_softplus(z2, mask) for z2 in zs]
        incls = [_dot(sp.astype(BF16), tri) for sp in sps]
        ws = [_sb_exp(zs[g], incls[g], state[2 * g], mask) for g in range(heads)]
        out = []
        for g, ln in enumerate(lanes):
            vb = v_ref[pl.ds(start, t), ln].astype(BF16)
            out += [state[2 * g] + jnp.sum(sps[g], axis=-1, keepdims=True), state[2 * g + 1] + _dot(ws[g], vb)]
        return tuple(out)

    state = (jnp.zeros((t, 1), F32), jnp.zeros((t, HEAD_DIM), F32)) * heads
    state = block(qi, state, c < r)
    state = lax.fori_loop(0, qi, lambda s, st: block(qi - 1 - s, st, None), state)
    for g, ln in enumerate(lanes):
        o_ref[:, ln] = (state[2 * g + 1] * _silu(za_ref[:, ln])).astype(o_ref.dtype)


def _attn_prompt(bias, q, k, v, za, *, n_batch, seq, heads=8):
    rows, width = k.shape
    n_heads = width // HEAD_DIM
    t = SB_BLOCK
    nq = seq // t
    q_spec = pl.BlockSpec((t, heads * HEAD_DIM), lambda b, h, qi: (b * nq + qi, h))
    kv_spec = pl.BlockSpec((seq, heads * HEAD_DIM), lambda b, h, qi: (b, h))
    return pl.pallas_call(
        functools.partial(_attn_prompt_kernel, t=t, heads=heads),
        grid=(n_batch, n_heads // heads, nq),
        in_specs=[pl.BlockSpec(memory_space=pltpu.SMEM), q_spec, kv_spec, kv_spec, q_spec],
        out_specs=q_spec,
        out_shape=jax.ShapeDtypeStruct((rows, width), BF16),
        compiler_params=_cparams(3),
        name="attn_prompt",
    )(bias, q, k, v, za)


def _suffix_sum_heads(x, n_heads):
    n = x.shape[1]
    lane = lax.broadcasted_iota(jnp.int32, x.shape, 1)
    s = n_heads
    while s < n:
        x = x + jnp.where(lane < n - s, pltpu.roll(x, n - s, 1), 0.0)
        s *= 2
    return x


def _bcast_heads(v, n_heads):
    lane = lax.broadcasted_iota(jnp.int32, v.shape, 1)
    v = jnp.where(lane < n_heads, v, 0.0)
    s = n_heads
    while s < v.shape[1]:
        v = v + pltpu.roll(v, s, 1)
        s *= 2
    return v


def _attn_sample_kernel(pt_ref, bias_ref, q_ref, kn_ref, vn_ref, za_ref, *rest, n_pages, page, n_heads):
    del pt_ref
    k_pages = rest[:n_pages]
    v_pages = rest[n_pages:2 * n_pages]
    o_ref, k_all, v_all = rest[2 * n_pages:]
    t_new = q_ref.shape[0]
    rows_pp = page * n_heads
    new_rows = kn_ref.shape[0]
    blk_new = HEAD_DIM
    past = n_pages * rows_pp
    for p in range(n_pages):
        k_all[p * rows_pp:(p + 1) * rows_pp, :] = k_pages[p][...].astype(BF16)
        v_all[p * rows_pp:(p + 1) * rows_pp, :] = v_pages[p][...].astype(BF16)
    pad = jnp.zeros((blk_new - new_rows, HEAD_DIM), F32)
    k_all[past:past + blk_new, :] = jnp.concatenate([kn_ref[...], pad], axis=0).astype(BF16)
    v_all[past:past + blk_new, :] = jnp.concatenate([vn_ref[...], pad], axis=0).astype(BF16)

    q = q_ref[...]
    q_rows = jnp.concatenate([q[:, h * HEAD_DIM:(h + 1) * HEAD_DIM] for h in range(n_heads)],
                             axis=0).astype(BF16)
    blocks = [n_pages] + list(range(n_pages - 1, -1, -1))
    sizes = [blk_new if j == n_pages else rows_pp for j in blocks]
    row = lax.broadcasted_iota(jnp.int32, (t_new, blk_new), 0)
    lanes = {n: lax.broadcasted_iota(jnp.int32, (t_new, n), 1) for n in set(sizes)}
    heads_of = {n: _imod(lanes[n], n_heads) for n in set(sizes)}
    biases = {}
    for n in set(sizes):
        bias = jnp.full((1, n), bias_ref[0, 0] * LOG2E, F32)
        for h in range(1, n_heads):
            bias = jnp.where(heads_of[n][0:1] == h, bias_ref[0, h] * LOG2E, bias)
        biases[n] = bias
    new_mask = _idiv(lanes[blk_new], n_heads) < row
    masks = [new_mask if j == n_pages else None for j in blocks]

    z2s = []
    for j, n in zip(blocks, sizes):
        s = _dot_nt(q_rows, k_all[j * rows_pp:j * rows_pp + n, :])
        z2 = s[0:t_new]
        for h in range(1, n_heads):
            z2 = jnp.where(heads_of[n] == h, s[h * t_new:(h + 1) * t_new], z2)
        z2s.append(z2 + biases[n])
    sps = [_sb_softplus(z2, m) for z2, m in zip(z2s, masks)]
    incls = [_suffix_sum_heads(sp, n_heads) for sp in sps]
    totals = [_bcast_heads(incl[:, 0:HEAD_DIM], n_heads) for incl in incls]
    carry = jnp.zeros((t_new, HEAD_DIM), F32)
    w_rows = []
    for i, n in enumerate(sizes):
        w = jnp.exp2(z2s[i] - incls[i] - jnp.concatenate([carry] * (n // HEAD_DIM), axis=1))
        if masks[i] is not None:
            w = jnp.where(masks[i], w, 0.0)
        w_rows.append(jnp.concatenate([jnp.where(heads_of[n] == h, w, 0.0) for h in range(n_heads)],
                                      axis=0).astype(BF16))
        carry = carry + totals[i]
    acc = jnp.zeros((n_heads * t_new, HEAD_DIM), F32)
    for j, n, wr in zip(blocks, sizes, w_rows):
        acc = acc + _dot(wr, v_all[j * rows_pp:j * rows_pp + n, :])
    o = jnp.concatenate([acc[h * t_new:(h + 1) * t_new] for h in range(n_heads)], axis=1)
    o_ref[...] = (o * _silu(za_ref[...])).astype(o_ref.dtype)


def _attn_sample(page_table, bias, q, k_new, v_new, za, cache_k, cache_v, *, n_phys):
    n_seq, t_new, width = q.shape
    n_pages = page_table.shape[1]
    n_heads = width // HEAD_DIM
    page = cache_k.shape[0] // (n_phys * n_heads)
    seq_spec = pl.BlockSpec((None, t_new, width), lambda n, pt: (n, 0, 0))
    new_spec = pl.BlockSpec((None, t_new * n_heads, HEAD_DIM), lambda n, pt: (n, 0, 0))

    def page_spec(p):
        return pl.BlockSpec((page * n_heads, HEAD_DIM), lambda n, pt: (pt[n, p], 0))

    grid_spec = pltpu.PrefetchScalarGridSpec(
        num_scalar_prefetch=1,
        grid=(n_seq,),
        in_specs=[pl.BlockSpec(memory_space=pltpu.SMEM), seq_spec, new_spec, new_spec, seq_spec]
                 + [page_spec(p) for p in range(n_pages)] * 2,
        out_specs=seq_spec,
        scratch_shapes=[pltpu.VMEM((n_pages * page * n_heads + HEAD_DIM, HEAD_DIM), BF16)] * 2,
    )
    return pl.pallas_call(
        functools.partial(_attn_sample_kernel, n_pages=n_pages, page=page, n_heads=n_heads),
        grid_spec=grid_spec,
        out_shape=jax.ShapeDtypeStruct((n_seq, t_new, width), F32),
        compiler_params=_cparams(1),
        name="attn_sample",
    )(page_table, bias, q, k_new, v_new, za, *([cache_k] * n_pages), *([cache_v] * n_pages))


def _outproj_ln_kernel(*refs, n_in, alpha):
    x_ref, gate_ref = refs[0], refs[1]
    acts = refs[2:2 + n_in]
    ws = refs[2 + n_in:2 + 2 * n_in]
    g_ref, b_ref, y_ref = refs[2 + 2 * n_in:]
    mix = _dot(acts[0][...].astype(BF16), ws[0][...])
    for a, w in zip(acts[1:], ws[1:]):
        mix = mix + _dot(a[...].astype(BF16), w[...])
    r = alpha * x_ref[...] + _gate_mix(mix, gate_ref[...])
    mu = jnp.mean(r, axis=-1, keepdims=True)
    rc = r - mu
    var = jnp.mean(rc * rc, axis=-1, keepdims=True)
    y_ref[...] = rc * lax.rsqrt(var + LN_EPS) * g_ref[...] + b_ref[...]


def _outproj_ln(x, gate, acts, w_out, ln_g, ln_b, *, tm, tiles_per_seq, alpha):
    rows, d = x.shape
    gm = gate.shape[-2]
    if gate.ndim == 3:
        mod_spec = pl.BlockSpec((None, gm, d), lambda i: (i // tiles_per_seq, 0, 0))
    else:
        mod_spec = pl.BlockSpec((gm, d), lambda i: (0, 0))
    n_in = len(acts)
    kw = w_out.shape[0] // n_in
    once = pl.Buffered(1)
    w_specs = [pl.BlockSpec((kw, d), functools.partial(lambda i, g: (g, 0), g=g), pipeline_mode=once)
               for g in range(n_in)]
    vec_spec = pl.BlockSpec((1, d), lambda i: (0, 0))
    return pl.pallas_call(
        functools.partial(_outproj_ln_kernel, n_in=n_in, alpha=alpha),
        grid=(rows // tm,),
        in_specs=[pl.BlockSpec((tm, d), lambda i: (i, 0)), mod_spec]
                 + [pl.BlockSpec((tm, kw), lambda i: (i, 0))] * n_in + w_specs + [vec_spec, vec_spec],
        out_specs=pl.BlockSpec((tm, d), lambda i: (i, 0)),
        out_shape=jax.ShapeDtypeStruct((rows, d), F32),
        compiler_params=_cparams(1),
        name="outproj_ln",
    )(x, gate, *acts, *([w_out] * n_in), ln_g.reshape(1, d), ln_b.reshape(1, d))


def _scan_rows(a_scr, b_scr, h_scr, h0):
    h = h0
    for r0 in range(0, a_scr.shape[0], SUBLANES):
        a8 = a_scr[r0:r0 + SUBLANES, :]
        b8 = b_scr[r0:r0 + SUBLANES, :]
        out = []
        for r in range(SUBLANES):
            h = a8[r:r + 1] * h + b8[r:r + 1]
            out.append(h)
        h_scr[r0:r0 + SUBLANES, :] = jnp.concatenate(out, axis=0)
    return h


def _lru_kernel(x_ref, shift_ref, scale_ref, state_ref, h0_ref, wx_ref, wz_ref, wconv_ref, bconv_ref,
                wga_ref, wgx_ref, bga_ref, bgx_ref, lam_ref,
                o_ref, tail_ref, hlast_ref, prev_scr, hcar_scr, a_scr, b_scr, h_scr, *, stride, tiles_per_seq):
    i = pl.program_id(0)

    @pl.when(i % tiles_per_seq == 0)
    def _():
        prev_scr[...] = state_ref[...]
        hcar_scr[...] = h0_ref[...]

    u = _modulate(x_ref[...], scale_ref[...], shift_ref[...]).astype(BF16)
    xr = _dot(u, wx_ref[...])
    prev = prev_scr[...]
    wc = wconv_ref[...]
    xc = wc[3:4] * xr + bconv_ref[...]
    for kk in range(1, 4):
        xc = xc + wc[3 - kk:4 - kk] * _shift_rows(xr, prev, kk * stride)
    new_prev = _next_prev(xr, prev)
    prev_scr[...] = new_prev
    tail_ref[...] = new_prev

    xcb = xc.astype(BF16)
    n_blocks, blk, _ = wga_ref.shape
    ga = jnp.concatenate([_dot(xcb[:, n * blk:(n + 1) * blk], wga_ref[n]) for n in range(n_blocks)], axis=1)
    gx = jnp.concatenate([_dot(xcb[:, n * blk:(n + 1) * blk], wgx_ref[n]) for n in range(n_blocks)], axis=1)
    lam = lam_ref[...]
    neg_log_sig = jnp.maximum(-lam, 0.0) + jnp.log1p(jnp.exp(-jnp.abs(lam)))
    log_a = (-LRU_C) * jax.nn.sigmoid(ga + bga_ref[...]) * neg_log_sig
    a = jnp.exp(log_a)
    bt = jnp.sqrt(jnp.tanh(-log_a) * (1.0 + a * a)) * jax.nn.sigmoid(gx + bgx_ref[...]) * xc

    if stride == x_ref.shape[0]:
        h = a * hcar_scr[...] + bt
        hcar_scr[...] = h
    else:
        a_scr[...] = a
        b_scr[...] = bt
        hcar_scr[...] = _scan_rows(a_scr, b_scr, h_scr, hcar_scr[...])
        h = h_scr[...]
    hlast_ref[...] = hcar_scr[...]
    z = _dot(u, wz_ref[...])
    o_ref[...] = (h * _silu(z)).astype(o_ref.dtype)


def _lru(x, shift, scale, state, h0, w_in, w_conv, b_conv, w_ga, w_gx, b_ga, b_gx, lam, *, tm, stride,
         tiles_per_seq):
    rows, d = x.shape
    p = state.shape[1]
    hr = h0.shape[1]
    n_i = rows // tm
    gm = shift.shape[-2]
    if shift.ndim == 3:
        mod_spec = pl.BlockSpec((None, gm, d), lambda i: (i // tiles_per_seq, 0, 0))
    else:
        mod_spec = pl.BlockSpec((gm, d), lambda i: (0, 0))
    once = pl.Buffered(1)
    vec_spec = pl.BlockSpec((1, d), lambda i: (0, 0))
    gate_spec = pl.BlockSpec(w_ga.shape, lambda i: (0, 0, 0), pipeline_mode=once)
    scan_scr = (tm, d) if stride != tm else (SUBLANES, 128)
    return pl.pallas_call(
        functools.partial(_lru_kernel, stride=stride, tiles_per_seq=tiles_per_seq),
        grid=(n_i,),
        in_specs=[pl.BlockSpec((tm, d), lambda i: (i, 0)), mod_spec, mod_spec,
                  pl.BlockSpec((None, p, d), lambda i: (i // tiles_per_seq, 0, 0)),
                  pl.BlockSpec((None, hr, d), lambda i: (i // tiles_per_seq, 0, 0)),
                  pl.BlockSpec((d, d), lambda i: (0, 0), pipeline_mode=once),
                  pl.BlockSpec((d, d), lambda i: (0, 1), pipeline_mode=once),
                  pl.BlockSpec((w_conv.shape[0], d), lambda i: (0, 0)), vec_spec,
                  gate_spec, gate_spec, vec_spec, vec_spec, vec_spec],
        out_specs=[pl.BlockSpec((tm, d), lambda i: (i, 0)),
                   pl.BlockSpec((None, p, d), lambda i: (i, 0, 0)),
                   pl.BlockSpec((None, hr, d), lambda i: (i, 0, 0))],
        out_shape=[jax.ShapeDtypeStruct((rows, d), BF16), jax.ShapeDtypeStruct((n_i, p, d), F32),
                   jax.ShapeDtypeStruct((n_i, hr, d), F32)],
        scratch_shapes=[pltpu.VMEM((p, d), F32), pltpu.VMEM((hr, d), F32)]
                       + [pltpu.VMEM(scan_scr, F32)] * 3,
        compiler_params=_cparams(1),
        name="lru",
    )(x, shift, scale, state, h0, w_in, w_in, w_conv, b_conv.reshape(1, d), w_ga, w_gx,
      b_ga.reshape(1, d), b_gx.reshape(1, d), lam.reshape(1, d))


def kernel(x_prompt, x_sample, c_prompt, c_sample, cache_k, cache_v, state_conv_b, state_conv_c, state_h, page_table, we_ada, be_ada, we_in, we_sb_bias, we_conv, we_out, ge_ln, be_ln, wo_ada, bo_ada, wo_in, wo_conv, bo_conv, wo_gate_a, bo_gate_a, wo_gate_x, bo_gate_x, wo_lambda, wo_out, go_ln, bo_ln):
    nb, seq, d = x_prompt.shape
    ns, dec, _ = x_sample.shape
    assert we_ada.shape[0] == 1 and wo_ada.shape[0] == 1, "one even and one odd layer"
    n_heads = we_sb_bias.shape[1]
    width = n_heads * HEAD_DIM
    alpha = (2.0 * (we_ada.shape[0] + wo_ada.shape[0])) ** 0.25
    tm = 512
    tm_in0 = 1024

    def to_time_major(a):
        return jnp.swapaxes(a, 0, 1).reshape(a.shape[1] * ns, a.shape[2])

    def to_seq_major(a, t):
        return jnp.swapaxes(a.reshape(t, ns, a.shape[-1]), 0, 1)

    pad = (-(nb + ns)) % SUBLANES
    c_all = jnp.concatenate([c_prompt, c_sample, jnp.zeros((pad, d), F32)], axis=0)
    mods = []
    for w_ada, b_ada in ((we_ada[0], be_ada[0]), (wo_ada[0], bo_ada[0])):
        m = _ada(c_all, w_ada, b_ada)
        parts = [m[:, g * d:(g + 1) * d] for g in range(3)]
        mods.append(([v[:nb].reshape(nb, 1, d) for v in parts], [v[nb:nb + ns] for v in parts]))

    xp = x_prompt.reshape(nb * seq, d)
    xs = to_time_major(x_sample)
    bias = we_sb_bias[0].reshape(1, n_heads)

    (shift_p, scale_p, gate_p), (shift_s, scale_s, gate_s) = mods[0]
    w_in0 = we_in[0].astype(BF16)
    w_out0 = we_out[0].astype(BF16)
    conv_k = we_conv.shape[1]
    qp, kp, vp, zap, obp, tailp = _inproj0(
        xp, shift_p, scale_p, jnp.zeros((nb, SUBLANES, width), F32), w_in0, we_conv[0],
        tm=tm_in0, tn=256, stride=1, tiles_per_seq=seq // tm_in0, q_dtype=BF16)
    oap = _attn_prompt(bias, qp, kp, vp, zap, n_batch=nb, seq=seq)
    yp = _outproj_ln(xp, gate_p, [oap, obp], w_out0, ge_ln[0], be_ln[0], tm=tm, tiles_per_seq=seq // tm,
                     alpha=alpha)
    conv_b_prompt = tailp.reshape(nb, seq // tm_in0, SUBLANES, width)[:, -1, SUBLANES - (conv_k - 1):][None]

    state_b = to_time_major(state_conv_b[0])[None]
    rows_s = ns * dec
    tiles_s = rows_s // tm
    qs, ks, vs, zas, obs, tails = _inproj0(
        xs, shift_s, scale_s, state_b, w_in0, we_conv[0],
        tm=tm, tn=256, stride=ns, tiles_per_seq=tiles_s, q_dtype=F32)
    k_sample = to_seq_major(ks, dec).reshape(1, ns, dec, n_heads, HEAD_DIM)
    v_sample = to_seq_major(vs, dec).reshape(1, ns, dec, n_heads, HEAD_DIM)
    n_phys = cache_k.shape[1]
    oas = _attn_sample(page_table, bias, to_seq_major(qs, dec), k_sample.reshape(ns, dec * n_heads, HEAD_DIM),
                       v_sample.reshape(ns, dec * n_heads, HEAD_DIM), to_seq_major(zas, dec),
                       cache_k.reshape(-1, HEAD_DIM), cache_v.reshape(-1, HEAD_DIM), n_phys=n_phys)
    ys = _outproj_ln(xs, gate_s, [to_time_major(oas), obs], w_out0, ge_ln[0], be_ln[0], tm=tm,
                     tiles_per_seq=tiles_s, alpha=alpha)
    conv_b_sample = to_seq_major(tails[-1], conv_k - 1)[None]

    (shift_p, scale_p, gate_p), (shift_s, scale_s, gate_s) = mods[1]
    w_in1 = wo_in[0].astype(BF16)
    w_out1 = wo_out[0].astype(BF16)
    w_ga = wo_gate_a[0].astype(BF16)
    w_gx = wo_gate_x[0].astype(BF16)
    conv_k = wo_conv.shape[1]
    lru_w = (w_in1, wo_conv[0], bo_conv[0], w_ga, w_gx, bo_gate_a[0], bo_gate_x[0], wo_lambda[0])
    tm1 = 256
    op, tailp, hp = _lru(yp, shift_p, scale_p, jnp.zeros((nb, SUBLANES, d), F32), jnp.zeros((nb, 1, d), F32),
                         *lru_w, tm=tm1, stride=1, tiles_per_seq=seq // tm1)
    yp = _outproj_ln(yp, gate_p, [op], w_out1, go_ln[0], bo_ln[0], tm=tm, tiles_per_seq=seq // tm, alpha=alpha)
    conv_c_prompt = tailp.reshape(nb, seq // tm1, SUBLANES, d)[:, -1, SUBLANES - (conv_k - 1):][None]
    h_prompt = hp.reshape(nb, seq // tm1, d)[:, -1][None]

    state_c = to_time_major(state_conv_c[0])[None]
    os_, tails, hs = _lru(ys, shift_s, scale_s, state_c, state_h[0][None], *lru_w, tm=ns, stride=ns,
                          tiles_per_seq=dec)
    ys = _outproj_ln(ys, gate_s, [os_], w_out1, go_ln[0], bo_ln[0], tm=tm, tiles_per_seq=tiles_s, alpha=alpha)
    conv_c_sample = to_seq_major(tails[-1], conv_k - 1)[None]
    h_sample = hs[-1][None]

    y_prompt = yp.reshape(nb, seq, d)
    y_sample = to_seq_major(ys, dec)
    k_prompt = kp.reshape(1, nb, seq, n_heads, HEAD_DIM)
    v_prompt = vp.reshape(1, nb, seq, n_heads, HEAD_DIM)
    return (y_prompt, y_sample, k_prompt, v_prompt, conv_b_prompt, conv_c_prompt, h_prompt,
            k_sample, v_sample, conv_b_sample, conv_c_sample, h_sample)
```

```python
import functools
import math

import jax
import jax.numpy as jnp
from jax import lax
from jax.experimental import pallas as pl
from jax.experimental.pallas import tpu as pltpu

F32 = jnp.float32
BF16 = jnp.bfloat16

HEAD_DIM = 128
SB_BLOCK = 256
LRU_C = 8.0
LN_EPS = 1e-5
LOG2E = math.log2(math.e)
Q_SCALE = LOG2E / math.sqrt(HEAD_DIM)
SUBLANES = 8
VMEM_LIMIT = 56 * 1024 * 1024


def _cparams(n_axes):
    return pltpu.CompilerParams(dimension_semantics=("arbitrary",) * n_axes, vmem_limit_bytes=VMEM_LIMIT)


def _silu(x):
    return x * jax.nn.sigmoid(x)


def _idiv(x, n):
    assert n & (n - 1) == 0
    return x >> (n.bit_length() - 1)


def _imod(x, n):
    assert n & (n - 1) == 0
    return x & (n - 1)


def _dot(a, b):
    return jnp.dot(a, b, preferred_element_type=F32)


def _dot_nt(a, b):
    return lax.dot_general(a, b, (((1,), (1,)), ((), ())), preferred_element_type=F32)


def _modulate(x, scale, shift):
    gm = scale.shape[0]
    if gm == 1:
        return x * (1.0 + scale) + shift
    rows, d = x.shape
    x3 = x.reshape(rows // gm, gm, d)
    return (x3 * (1.0 + scale)[None] + shift[None]).reshape(rows, d)


def _gate_mix(mix, gate):
    gm = gate.shape[0]
    if gm == 1:
        return mix * (1.0 + gate)
    rows, d = mix.shape
    return (mix.reshape(rows // gm, gm, d) * (1.0 + gate)[None]).reshape(rows, d)


def _shift_rows(cur, prev, d):
    tm, p = cur.shape[0], prev.shape[0]
    if d % SUBLANES == 0:
        if d >= tm:
            return prev[p - d:p - d + tm]
        return jnp.concatenate([prev[p - d:], cur[:tm - d]], axis=0)
    assert d < SUBLANES <= p
    rolled = pltpu.roll(cur, d, 0)
    prev_rolled = pltpu.roll(prev[p - SUBLANES:], d, 0)
    row = lax.broadcasted_iota(jnp.int32, prev_rolled.shape, 0)
    head = jnp.where(row < d, prev_rolled, rolled[:SUBLANES])
    return jnp.concatenate([head, rolled[SUBLANES:]], axis=0)


def _next_prev(cur, prev):
    tm, p = cur.shape[0], prev.shape[0]
    if tm >= p:
        return cur[tm - p:]
    return jnp.concatenate([prev[tm:], cur], axis=0)


def _ada_kernel(c_ref, w_ref, b_ref, o_ref):
    s = _silu(c_ref[...]).astype(BF16)
    o_ref[...] = _dot(s, w_ref[...].astype(BF16)) + b_ref[...]


def _ada(c_all, w_ada, b_ada, tn=1024):
    rows, d = c_all.shape
    n = w_ada.shape[1]
    return pl.pallas_call(
        _ada_kernel,
        grid=(n // tn,),
        in_specs=[pl.BlockSpec((rows, d), lambda j: (0, 0)),
                  pl.BlockSpec((d, tn), lambda j: (0, j)),
                  pl.BlockSpec((1, tn), lambda j: (0, j))],
        out_specs=pl.BlockSpec((rows, tn), lambda j: (0, j)),
        out_shape=jax.ShapeDtypeStruct((rows, n), F32),
        compiler_params=_cparams(1),
        name="ada",
    )(c_all, w_ada, b_ada.reshape(1, n))


def _inproj0_kernel(x_ref, shift_ref, scale_ref, state_ref, wq, wk, wv, wza, wbg, wcg, wxi, wzb, wconv_ref,
                    q_ref, k_ref, v_ref, za_ref, ob_ref, tail_ref, u_scr, carry_scr, *, stride, tiles_per_seq):
    i = pl.program_id(0)
    j = pl.program_id(1)

    @pl.when(j == 0)
    def _():
        u_scr[...] = _modulate(x_ref[...], scale_ref[...], shift_ref[...]).astype(BF16)

    @pl.when(i % tiles_per_seq == 0)
    def _():
        carry_scr[j] = state_ref[...]

    u = u_scr[...]
    q_ref[...] = (_dot(u, wq[...]) * Q_SCALE).astype(q_ref.dtype)
    k_ref[...] = _dot(u, wk[...])
    v_ref[...] = _dot(u, wv[...])
    za_ref[...] = _dot(u, wza[...])
    ci = _dot(u, wcg[...]) * _dot(u, wxi[...])
    prev = carry_scr[j]
    wc = wconv_ref[...]
    conv = wc[2:3] * ci + wc[1:2] * _shift_rows(ci, prev, stride) + wc[0:1] * _shift_rows(ci, prev, 2 * stride)
    ob_ref[...] = (_dot(u, wbg[...]) * conv * _silu(_dot(u, wzb[...]))).astype(ob_ref.dtype)
    new_prev = _next_prev(ci, prev)
    carry_scr[j] = new_prev
    tail_ref[...] = new_prev


def _inproj0(x, shift, scale, state, w_in, w_conv, *, tm, tn, stride, tiles_per_seq, q_dtype):
    rows, d = x.shape
    c = w_conv.shape[1]
    p = state.shape[1]
    n_i, n_j = rows // tm, c // tn
    gm = shift.shape[-2]
    if shift.ndim == 3:
        mod_spec = pl.BlockSpec((None, gm, d), lambda i, j: (i // tiles_per_seq, 0, 0))
    else:
        mod_spec = pl.BlockSpec((gm, d), lambda i, j: (0, 0))

    def w_spec(g):
        return pl.BlockSpec((d, tn), lambda i, j: (0, g * n_j + j))

    act_spec = pl.BlockSpec((tm, tn), lambda i, j: (i, j))
    outs = pl.pallas_call(
        functools.partial(_inproj0_kernel, stride=stride, tiles_per_seq=tiles_per_seq),
        grid=(n_i, n_j),
        in_specs=[pl.BlockSpec((tm, d), lambda i, j: (i, 0)), mod_spec, mod_spec,
                  pl.BlockSpec((None, p, tn), lambda i, j: (i // tiles_per_seq, 0, j))]
                 + [w_spec(g) for g in range(8)]
                 + [pl.BlockSpec((w_conv.shape[0], tn), lambda i, j: (0, j))],
        out_specs=[act_spec] * 5 + [pl.BlockSpec((None, p, tn), lambda i, j: (i, 0, j))],
        out_shape=[jax.ShapeDtypeStruct((rows, c), q_dtype)] + [jax.ShapeDtypeStruct((rows, c), F32)] * 3
                  + [jax.ShapeDtypeStruct((rows, c), BF16), jax.ShapeDtypeStruct((n_i, p, c), F32)],
        scratch_shapes=[pltpu.VMEM((tm, d), BF16), pltpu.VMEM((n_j, p, tn), F32)],
        compiler_params=_cparams(2),
        name="inproj0",
    )(x, shift, scale, state, *([w_in] * 8), w_conv)
    return outs


def _sb_softplus(z2, mask):
    neg_abs = lax.bitcast_convert_type(lax.bitcast_convert_type(z2, jnp.uint32) | jnp.uint32(0x80000000), F32)
    sp = jnp.maximum(z2, 0.0) + jnp.log2(1.0 + jnp.exp2(neg_abs))
    return sp if mask is None else jnp.where(mask, sp, 0.0)


def _sb_exp(z2, incl, carry, mask):
    w = jnp.exp2(z2 - incl - carry)
    return (w if mask is None else jnp.where(mask, w, 0.0)).astype(BF16)


def _tri(t):
    r = lax.broadcasted_iota(jnp.int32, (t, t), 0)
    c = lax.broadcasted_iota(jnp.int32, (t, t), 1)
    return jnp.where(r >= c, 1.0, 0.0).astype(BF16)


def _suffix_sum_heads(x, n_heads):
    n = x.shape[1]
    lane = lax.broadcasted_iota(jnp.int32, x.shape, 1)
    s = n_heads
    while s < n:
        x = x + jnp.where(lane < n - s, pltpu.roll(x, n - s, 1), 0.0)
        s *= 2
    return x


def _bcast_heads(v, n_heads):
    lane = lax.broadcasted_iota(jnp.int32, v.shape, 1)
    v = jnp.where(lane < n_heads, v, 0.0)
    s = n_heads
    while s < v.shape[1]:
        v = v + pltpu.roll(v, s, 1)
        s *= 2
    return v


class _SampleAttn:
    def __init__(self, bias_ref, q_ref, kn_ref, vn_ref, k_pages, v_pages, n_heads):
        self.n_heads = n_heads
        self.t_new = t_new = q_ref.shape[0]
        self.k_pages, self.v_pages, self.kn_ref, self.vn_ref = k_pages, v_pages, kn_ref, vn_ref
        n_pages = len(k_pages)
        rows_pp = k_pages[0].shape[0]
        self.blocks = [n_pages] + list(range(n_pages - 1, -1, -1))
        self.sizes = [HEAD_DIM if j == n_pages else rows_pp for j in self.blocks]
        q = q_ref[...]
        self.q_rows = jnp.concatenate([q[:, h * HEAD_DIM:(h + 1) * HEAD_DIM] for h in range(n_heads)],
                                      axis=0).astype(BF16)
        row = lax.broadcasted_iota(jnp.int32, (t_new, HEAD_DIM), 0)
        lanes = {n: lax.broadcasted_iota(jnp.int32, (t_new, n), 1) for n in set(self.sizes)}
        self.heads_of = {n: _imod(lanes[n], n_heads) for n in set(self.sizes)}
        self.biases = {}
        for n in set(self.sizes):
            bias = jnp.full((1, n), bias_ref[0, 0] * LOG2E, F32)
            for h in range(1, n_heads):
                bias = jnp.where(self.heads_of[n][0:1] == h, bias_ref[0, h] * LOG2E, bias)
            self.biases[n] = bias
        new_mask = _idiv(lanes[HEAD_DIM], n_heads) < row
        self.masks = [new_mask if j == n_pages else None for j in self.blocks]

    def _block(self, pages, new_ref, j):
        if j < len(pages):
            return pages[j][...].astype(BF16)
        pad = jnp.zeros((HEAD_DIM - new_ref.shape[0], HEAD_DIM), F32)
        return jnp.concatenate([new_ref[...], pad], axis=0).astype(BF16)

    def logits(self):
        t_new = self.t_new
        self.z2s = []
        for j, n in zip(self.blocks, self.sizes):
            s = _dot_nt(self.q_rows, self._block(self.k_pages, self.kn_ref, j))
            z2 = s[0:t_new]
            for h in range(1, self.n_heads):
                z2 = jnp.where(self.heads_of[n] == h, s[h * t_new:(h + 1) * t_new], z2)
            self.z2s.append(z2 + self.biases[n])

    def weights(self):
        sps = [_sb_softplus(z2, m) for z2, m in zip(self.z2s, self.masks)]
        incls = [_suffix_sum_heads(sp, self.n_heads) for sp in sps]
        totals = [_bcast_heads(incl[:, 0:HEAD_DIM], self.n_heads) for incl in incls]
        carry = jnp.zeros((self.t_new, HEAD_DIM), F32)
        self.w_rows = []
        for i, n in enumerate(self.sizes):
            w = jnp.exp2(self.z2s[i] - incls[i] - jnp.concatenate([carry] * (n // HEAD_DIM), axis=1))
            if self.masks[i] is not None:
                w = jnp.where(self.masks[i], w, 0.0)
            self.w_rows.append(jnp.concatenate([jnp.where(self.heads_of[n] == h, w, 0.0)
                                                for h in range(self.n_heads)], axis=0).astype(BF16))
            carry = carry + totals[i]

    def output(self, za_ref, o_ref):
        t_new = self.t_new
        acc = jnp.zeros((self.n_heads * t_new, HEAD_DIM), F32)
        for j, wr in zip(self.blocks, self.w_rows):
            acc = acc + _dot(wr, self._block(self.v_pages, self.vn_ref, j))
        o = jnp.concatenate([acc[h * t_new:(h + 1) * t_new] for h in range(self.n_heads)], axis=1)
        o_ref[...] = (o * _silu(za_ref[...])).astype(o_ref.dtype)


def _pick(cond, a, b):
    if isinstance(cond, bool):
        return a if cond else b
    return jnp.where(cond, a, b)


class _PromptAttn:
    def __init__(self, bias, q_ref, k_ref, v_ref, s, t):
        self.q_ref, self.k_ref, self.v_ref, self.s, self.t = q_ref, k_ref, v_ref, s, t
        self.nq = nq = q_ref.shape[0] // t
        self.n_items = nq + 1
        self.bias = bias * LOG2E
        self.in_a = [True if i == 0 else False if i >= nq // 2 else i <= s for i in range(self.n_items)]
        self.qrow = [pl.multiple_of(_pick(self.in_a[i], s, nq - 1 - s) * t, t) for i in range(self.n_items)]
        self.krow = [pl.multiple_of(_pick(self.in_a[i], s - i, nq - i) * t, t) for i in range(self.n_items)]
        r = lax.broadcasted_iota(jnp.int32, (t, t), 0)
        c = lax.broadcasted_iota(jnp.int32, (t, t), 1)
        self.masks = []
        for i in range(self.n_items):
            if i == 0:
                self.masks.append(c < r)
            elif i <= nq // 2:
                self.masks.append(c < r + jnp.where(i == s + 1, 0, t))
            else:
                self.masks.append(None)

    def logits(self):
        t = self.t
        self.zs = [_dot_nt(self.q_ref[pl.ds(self.qrow[i], t), :], self.k_ref[pl.ds(self.krow[i], t), :].astype(BF16))
                   + self.bias for i in range(self.n_items)]

    def weights(self):
        t = self.t
        tri = _tri(t)
        sps = [_sb_softplus(z2, m) for z2, m in zip(self.zs, self.masks)]
        incls = [_dot(sp.astype(BF16), tri) for sp in sps]
        run_a = jnp.zeros((t, 1), F32)
        run_b = jnp.zeros((t, 1), F32)
        self.ws = []
        for i in range(self.n_items):
            rowsum = jnp.sum(sps[i], axis=-1, keepdims=True)
            carry = _pick(self.in_a[i], run_a, run_b)
            self.ws.append(_sb_exp(self.zs[i], incls[i], carry, self.masks[i]))
            run_a = run_a + _pick(self.in_a[i], rowsum, 0.0)
            run_b = run_b + _pick(self.in_a[i], 0.0, rowsum)

    def output(self, za_ref, o_ref):
        t, s = self.t, self.s
        acc_a = jnp.zeros((t, HEAD_DIM), F32)
        acc_b = jnp.zeros((t, HEAD_DIM), F32)
        for i in range(self.n_items):
            pv = _dot(self.ws[i], self.v_ref[pl.ds(self.krow[i], t), :].astype(BF16))
            acc_a = acc_a + _pick(self.in_a[i], pv, 0.0)
            acc_b = acc_b + _pick(self.in_a[i], 0.0, pv)
        for acc, tile in ((acc_a, s), (acc_b, self.nq - 1 - s)):
            rows = pl.ds(pl.multiple_of(tile * t, t), t)
            o_ref[rows, :] = (acc * _silu(za_ref[rows, :])).astype(o_ref.dtype)


def _attn_kernel(pt_ref, bias_ref, q_ref, kn_ref, vn_ref, za_ref, *rest, n_pages, n_heads, t):
    del pt_ref
    k_pages = rest[:n_pages]
    v_pages = rest[n_pages:2 * n_pages]
    qp_ref, kp_ref, vp_ref, zap_ref, o_ref, op_ref = rest[2 * n_pages:]
    nq = qp_ref.shape[0] // t
    n = pl.program_id(0)
    pairs = nq // 2
    head = _imod(_idiv(n, pairs), n_heads)
    sample = _SampleAttn(bias_ref, q_ref, kn_ref, vn_ref, k_pages, v_pages, n_heads)
    prompt = _PromptAttn(bias_ref[0, head], qp_ref, kp_ref, vp_ref, _imod(n, pairs), t)
    sample.logits()
    prompt.logits()
    sample.weights()
    prompt.weights()
    sample.output(za_ref, o_ref)
    prompt.output(zap_ref, op_ref)


def _attn(page_table, bias, q, k_new, v_new, za, cache_k, cache_v, qp, kp, vp, zap, *, n_phys, n_batch, seq):
    n_seq, t_new, width = q.shape
    n_pages = page_table.shape[1]
    n_heads = width // HEAD_DIM
    page = cache_k.shape[0] // (n_phys * n_heads)
    t = SB_BLOCK
    pairs = seq // t // 2
    assert n_seq == n_batch * n_heads * pairs, "one (prompt sequence, head, query-tile pair) per sample sequence"
    seq_spec = pl.BlockSpec((None, t_new, width), lambda n, pt: (n, 0, 0))
    new_spec = pl.BlockSpec((None, t_new * n_heads, HEAD_DIM), lambda n, pt: (n, 0, 0))
    head_spec = pl.BlockSpec((seq, HEAD_DIM), lambda n, pt: (n // (n_heads * pairs), (n // pairs) % n_heads))

    def page_spec(p):
        return pl.BlockSpec((page * n_heads, HEAD_DIM), lambda n, pt: (pt[n, p], 0))

    grid_spec = pltpu.PrefetchScalarGridSpec(
        num_scalar_prefetch=1,
        grid=(n_seq,),
        in_specs=[pl.BlockSpec(memory_space=pltpu.SMEM), seq_spec, new_spec, new_spec, seq_spec]
                 + [page_spec(p) for p in range(n_pages)] * 2 + [head_spec] * 4,
        out_specs=[seq_spec, head_spec],
    )
    return pl.pallas_call(
        functools.partial(_attn_kernel, n_pages=n_pages, n_heads=n_heads, t=t),
        grid_spec=grid_spec,
        out_shape=[jax.ShapeDtypeStruct((n_seq, t_new, width), F32), jax.ShapeDtypeStruct(qp.shape, BF16)],
        compiler_params=_cparams(1),
        name="attn",
    )(page_table, bias, q, k_new, v_new, za, *([cache_k] * n_pages), *([cache_v] * n_pages), qp, kp, vp, zap)


def _outproj_ln_kernel(*refs, n_in, alpha):
    x_ref, gate_ref = refs[0], refs[1]
    acts = refs[2:2 + n_in]
    ws = refs[2 + n_in:2 + 2 * n_in]
    g_ref, b_ref, y_ref = refs[2 + 2 * n_in:]
    mix = _dot(acts[0][...].astype(BF16), ws[0][...])
    for a, w in zip(acts[1:], ws[1:]):
        mix = mix + _dot(a[...].astype(BF16), w[...])
    r = alpha * x_ref[...] + _gate_mix(mix, gate_ref[...])
    mu = jnp.mean(r, axis=-1, keepdims=True)
    rc = r - mu
    var = jnp.mean(rc * rc, axis=-1, keepdims=True)
    y_ref[...] = rc * lax.rsqrt(var + LN_EPS) * g_ref[...] + b_ref[...]


def _outproj_ln(x, gate, acts, w_out, ln_g, ln_b, *, tm, tiles_per_seq, alpha):
    rows, d = x.shape
    gm = gate.shape[-2]
    if gate.ndim == 3:
        mod_spec = pl.BlockSpec((None, gm, d), lambda i: (i // tiles_per_seq, 0, 0))
    else:
        mod_spec = pl.BlockSpec((gm, d), lambda i: (0, 0))
    n_in = len(acts)
    kw = w_out.shape[0] // n_in
    once = pl.Buffered(1)
    w_specs = [pl.BlockSpec((kw, d), functools.partial(lambda i, g: (g, 0), g=g), pipeline_mode=once)
               for g in range(n_in)]
    vec_spec = pl.BlockSpec((1, d), lambda i: (0, 0))
    return pl.pallas_call(
        functools.partial(_outproj_ln_kernel, n_in=n_in, alpha=alpha),
        grid=(rows // tm,),
        in_specs=[pl.BlockSpec((tm, d), lambda i: (i, 0)), mod_spec]
                 + [pl.BlockSpec((tm, kw), lambda i: (i, 0))] * n_in + w_specs + [vec_spec, vec_spec],
        out_specs=pl.BlockSpec((tm, d), lambda i: (i, 0)),
        out_shape=jax.ShapeDtypeStruct((rows, d), F32),
        compiler_params=_cparams(1),
        name="outproj_ln",
    )(x, gate, *acts, *([w_out] * n_in), ln_g.reshape(1, d), ln_b.reshape(1, d))


def _scan_rows(a_scr, b_scr, h_scr, h0):
    h = h0
    for r0 in range(0, a_scr.shape[0], SUBLANES):
        a8 = a_scr[r0:r0 + SUBLANES, :]
        b8 = b_scr[r0:r0 + SUBLANES, :]
        out = []
        for r in range(SUBLANES):
            h = a8[r:r + 1] * h + b8[r:r + 1]
            out.append(h)
        h_scr[r0:r0 + SUBLANES, :] = jnp.concatenate(out, axis=0)
    return h


def _lru_kernel(x_ref, shift_ref, scale_ref, state_ref, h0_ref, wx_ref, wz_ref, wconv_ref, bconv_ref,
                wga_ref, wgx_ref, bga_ref, bgx_ref, lam_ref,
                o_ref, tail_ref, hlast_ref, prev_scr, hcar_scr, a_scr, b_scr, h_scr, *, stride, tiles_per_seq):
    i = pl.program_id(0)

    @pl.when(i % tiles_per_seq == 0)
    def _():
        prev_scr[...] = state_ref[...]
        hcar_scr[...] = h0_ref[...]

    u = _modulate(x_ref[...], scale_ref[...], shift_ref[...]).astype(BF16)
    xr = _dot(u, wx_ref[...])
    prev = prev_scr[...]
    wc = wconv_ref[...]
    xc = wc[3:4] * xr + bconv_ref[...]
    for kk in range(1, 4):
        xc = xc + wc[3 - kk:4 - kk] * _shift_rows(xr, prev, kk * stride)
    new_prev = _next_prev(xr, prev)
    prev_scr[...] = new_prev
    tail_ref[...] = new_prev

    xcb = xc.astype(BF16)
    n_blocks, blk, _ = wga_ref.shape
    ga = jnp.concatenate([_dot(xcb[:, n * blk:(n + 1) * blk], wga_ref[n]) for n in range(n_blocks)], axis=1)
    gx = jnp.concatenate([_dot(xcb[:, n * blk:(n + 1) * blk], wgx_ref[n]) for n in range(n_blocks)], axis=1)
    lam = lam_ref[...]
    neg_log_sig = jnp.maximum(-lam, 0.0) + jnp.log1p(jnp.exp(-jnp.abs(lam)))
    log_a = (-LRU_C) * jax.nn.sigmoid(ga + bga_ref[...]) * neg_log_sig
    a = jnp.exp(log_a)
    bt = jnp.sqrt(jnp.tanh(-log_a) * (1.0 + a * a)) * jax.nn.sigmoid(gx + bgx_ref[...]) * xc

    if stride == x_ref.shape[0]:
        h = a * hcar_scr[...] + bt
        hcar_scr[...] = h
    else:
        a_scr[...] = a
        b_scr[...] = bt
        hcar_scr[...] = _scan_rows(a_scr, b_scr, h_scr, hcar_scr[...])
        h = h_scr[...]
    hlast_ref[...] = hcar_scr[...]
    z = _dot(u, wz_ref[...])
    o_ref[...] = (h * _silu(z)).astype(o_ref.dtype)


def _lru(x, shift, scale, state, h0, w_in, w_conv, b_conv, w_ga, w_gx, b_ga, b_gx, lam, *, tm, stride,
         tiles_per_seq):
    rows, d = x.shape
    p = state.shape[1]
    hr = h0.shape[1]
    n_i = rows // tm
    gm = shift.shape[-2]
    if shift.ndim == 3:
        mod_spec = pl.BlockSpec((None, gm, d), lambda i: (i // tiles_per_seq, 0, 0))
    else:
        mod_spec = pl.BlockSpec((gm, d), lambda i: (0, 0))
    once = pl.Buffered(1)
    vec_spec = pl.BlockSpec((1, d), lambda i: (0, 0))
    gate_spec = pl.BlockSpec(w_ga.shape, lambda i: (0, 0, 0), pipeline_mode=once)
    scan_scr = (tm, d) if stride != tm else (SUBLANES, 128)
    return pl.pallas_call(
        functools.partial(_lru_kernel, stride=stride, tiles_per_seq=tiles_per_seq),
        grid=(n_i,),
        in_specs=[pl.BlockSpec((tm, d), lambda i: (i, 0)), mod_spec, mod_spec,
                  pl.BlockSpec((None, p, d), lambda i: (i // tiles_per_seq, 0, 0)),
                  pl.BlockSpec((None, hr, d), lambda i: (i // tiles_per_seq, 0, 0)),
                  pl.BlockSpec((d, d), lambda i: (0, 0), pipeline_mode=once),
                  pl.BlockSpec((d, d), lambda i: (0, 1), pipeline_mode=once),
                  pl.BlockSpec((w_conv.shape[0], d), lambda i: (0, 0)), vec_spec,
                  gate_spec, gate_spec, vec_spec, vec_spec, vec_spec],
        out_specs=[pl.BlockSpec((tm, d), lambda i: (i, 0)),
                   pl.BlockSpec((None, p, d), lambda i: (i, 0, 0)),
                   pl.BlockSpec((None, hr, d), lambda i: (i, 0, 0))],
        out_shape=[jax.ShapeDtypeStruct((rows, d), BF16), jax.ShapeDtypeStruct((n_i, p, d), F32),
                   jax.ShapeDtypeStruct((n_i, hr, d), F32)],
        scratch_shapes=[pltpu.VMEM((p, d), F32), pltpu.VMEM((hr, d), F32)]
                       + [pltpu.VMEM(scan_scr, F32)] * 3,
        compiler_params=_cparams(1),
        name="lru",
    )(x, shift, scale, state, h0, w_in, w_in, w_conv, b_conv.reshape(1, d), w_ga, w_gx,
      b_ga.reshape(1, d), b_gx.reshape(1, d), lam.reshape(1, d))


def kernel(x_prompt, x_sample, c_prompt, c_sample, cache_k, cache_v, state_conv_b, state_conv_c, state_h, page_table, we_ada, be_ada, we_in, we_sb_bias, we_conv, we_out, ge_ln, be_ln, wo_ada, bo_ada, wo_in, wo_conv, bo_conv, wo_gate_a, bo_gate_a, wo_gate_x, bo_gate_x, wo_lambda, wo_out, go_ln, bo_ln):
    nb, seq, d = x_prompt.shape
    ns, dec, _ = x_sample.shape
    assert we_ada.shape[0] == 1 and wo_ada.shape[0] == 1, "one even and one odd layer"
    n_heads = we_sb_bias.shape[1]
    width = n_heads * HEAD_DIM
    alpha = (2.0 * (we_ada.shape[0] + wo_ada.shape[0])) ** 0.25
    tm = 512
    tm_in0 = 1024

    def to_time_major(a):
        return jnp.swapaxes(a, 0, 1).reshape(a.shape[1] * ns, a.shape[2])

    def to_seq_major(a, t):
        return jnp.swapaxes(a.reshape(t, ns, a.shape[-1]), 0, 1)

    pad = (-(nb + ns)) % SUBLANES
    c_all = jnp.concatenate([c_prompt, c_sample, jnp.zeros((pad, d), F32)], axis=0)
    mods = []
    for w_ada, b_ada in ((we_ada[0], be_ada[0]), (wo_ada[0], bo_ada[0])):
        m = _ada(c_all, w_ada, b_ada)
        parts = [m[:, g * d:(g + 1) * d] for g in range(3)]
        mods.append(([v[:nb].reshape(nb, 1, d) for v in parts], [v[nb:nb + ns] for v in parts]))

    xp = x_prompt.reshape(nb * seq, d)
    xs = to_time_major(x_sample)
    bias = we_sb_bias[0].reshape(1, n_heads)

    (shift_p, scale_p, gate_p), (shift_s, scale_s, gate_s) = mods[0]
    w_in0 = we_in[0].astype(BF16)
    w_out0 = we_out[0].astype(BF16)
    conv_k = we_conv.shape[1]
    qp, kp, vp, zap, obp, tailp = _inproj0(
        xp, shift_p, scale_p, jnp.zeros((nb, SUBLANES, width), F32), w_in0, we_conv[0],
        tm=tm_in0, tn=256, stride=1, tiles_per_seq=seq // tm_in0, q_dtype=BF16)
    conv_b_prompt = tailp.reshape(nb, seq // tm_in0, SUBLANES, width)[:, -1, SUBLANES - (conv_k - 1):][None]

    state_b = to_time_major(state_conv_b[0])[None]
    rows_s = ns * dec
    tiles_s = rows_s // tm
    qs, ks, vs, zas, obs, tails = _inproj0(
        xs, shift_s, scale_s, state_b, w_in0, we_conv[0],
        tm=tm, tn=256, stride=ns, tiles_per_seq=tiles_s, q_dtype=F32)
    k_sample = to_seq_major(ks, dec).reshape(1, ns, dec, n_heads, HEAD_DIM)
    v_sample = to_seq_major(vs, dec).reshape(1, ns, dec, n_heads, HEAD_DIM)
    n_phys = cache_k.shape[1]
    oas, oap = _attn(page_table, bias, to_seq_major(qs, dec), k_sample.reshape(ns, dec * n_heads, HEAD_DIM),
                     v_sample.reshape(ns, dec * n_heads, HEAD_DIM), to_seq_major(zas, dec),
                     cache_k.reshape(-1, HEAD_DIM), cache_v.reshape(-1, HEAD_DIM), qp, kp, vp, zap,
                     n_phys=n_phys, n_batch=nb, seq=seq)
    yp = _outproj_ln(xp, gate_p, [oap, obp], w_out0, ge_ln[0], be_ln[0], tm=tm, tiles_per_seq=seq // tm,
                     alpha=alpha)
    ys = _outproj_ln(xs, gate_s, [to_time_major(oas), obs], w_out0, ge_ln[0], be_ln[0], tm=tm,
                     tiles_per_seq=tiles_s, alpha=alpha)
    conv_b_sample = to_seq_major(tails[-1], conv_k - 1)[None]

    (shift_p, scale_p, gate_p), (shift_s, scale_s, gate_s) = mods[1]
    w_in1 = wo_in[0].astype(BF16)
    w_out1 = wo_out[0].astype(BF16)
    w_ga = wo_gate_a[0].astype(BF16)
    w_gx = wo_gate_x[0].astype(BF16)
    conv_k = wo_conv.shape[1]
    lru_w = (w_in1, wo_conv[0], bo_conv[0], w_ga, w_gx, bo_gate_a[0], bo_gate_x[0], wo_lambda[0])
    tm1 = 256
    op, tailp, hp = _lru(yp, shift_p, scale_p, jnp.zeros((nb, SUBLANES, d), F32), jnp.zeros((nb, 1, d), F32),
                         *lru_w, tm=tm1, stride=1, tiles_per_seq=seq // tm1)
    yp = _outproj_ln(yp, gate_p, [op], w_out1, go_ln[0], bo_ln[0], tm=tm, tiles_per_seq=seq // tm, alpha=alpha)
    conv_c_prompt = tailp.reshape(nb, seq // tm1, SUBLANES, d)[:, -1, SUBLANES - (conv_k - 1):][None]
    h_prompt = hp.reshape(nb, seq // tm1, d)[:, -1][None]

    state_c = to_time_major(state_conv_c[0])[None]
    os_, tails, hs = _lru(ys, shift_s, scale_s, state_c, state_h[0][None], *lru_w, tm=ns, stride=ns,
                          tiles_per_seq=dec)
    ys = _outproj_ln(ys, gate_s, [os_], w_out1, go_ln[0], bo_ln[0], tm=tm, tiles_per_seq=tiles_s, alpha=alpha)
    conv_c_sample = to_seq_major(tails[-1], conv_k - 1)[None]
    h_sample = hs[-1][None]

    y_prompt = yp.reshape(nb, seq, d)
    y_sample = to_seq_major(ys, dec)
    k_prompt = kp.reshape(1, nb, seq, n_heads, HEAD_DIM)
    v_prompt = vp.reshape(1, nb, seq, n_heads, HEAD_DIM)
    return (y_prompt, y_sample, k_prompt, v_prompt, conv_b_prompt, conv_c_prompt, h_prompt,
            k_sample, v_sample, conv_b_sample, conv_c_sample, h_sample)
```

```python
import functools
import math

import jax
import jax.numpy as jnp
from jax import lax
from jax.experimental import pallas as pl
from jax.experimental.pallas import tpu as pltpu

F32 = jnp.float32
BF16 = jnp.bfloat16

HEAD_DIM = 128
SB_BLOCK = 256
LRU_C = 8.0
LN_EPS = 1e-5
LOG2E = math.log2(math.e)
Q_SCALE = LOG2E / math.sqrt(HEAD_DIM)
SUBLANES = 8
VMEM_LIMIT = 58 * 1024 * 1024


def _cparams(n_axes):
    return pltpu.CompilerParams(dimension_semantics=("arbitrary",) * n_axes, vmem_limit_bytes=VMEM_LIMIT)


def _silu(x):
    return x * jax.nn.sigmoid(x)


def _idiv(x, n):
    assert n & (n - 1) == 0
    return x >> (n.bit_length() - 1)


def _imod(x, n):
    assert n & (n - 1) == 0
    return x & (n - 1)


def _dot(a, b):
    return jnp.dot(a, b, preferred_element_type=F32)


def _dot_nt(a, b):
    return lax.dot_general(a, b, (((1,), (1,)), ((), ())), preferred_element_type=F32)


def _modulate(x, scale, shift):
    gm = scale.shape[0]
    if gm == 1:
        return x * (1.0 + scale) + shift
    rows, d = x.shape
    x3 = x.reshape(rows // gm, gm, d)
    return (x3 * (1.0 + scale)[None] + shift[None]).reshape(rows, d)


def _gate_mix(mix, gate):
    gm = gate.shape[0]
    if gm == 1:
        return mix * (1.0 + gate)
    rows, d = mix.shape
    return (mix.reshape(rows // gm, gm, d) * (1.0 + gate)[None]).reshape(rows, d)


def _shift_rows(cur, prev, d):
    tm, p = cur.shape[0], prev.shape[0]
    if d % SUBLANES == 0:
        if d >= tm:
            return prev[p - d:p - d + tm]
        return jnp.concatenate([prev[p - d:], cur[:tm - d]], axis=0)
    assert d < SUBLANES <= p
    rolled = pltpu.roll(cur, d, 0)
    prev_rolled = pltpu.roll(prev[p - SUBLANES:], d, 0)
    row = lax.broadcasted_iota(jnp.int32, prev_rolled.shape, 0)
    head = jnp.where(row < d, prev_rolled, rolled[:SUBLANES])
    return jnp.concatenate([head, rolled[SUBLANES:]], axis=0)


def _next_prev(cur, prev):
    tm, p = cur.shape[0], prev.shape[0]
    if tm >= p:
        return cur[tm - p:]
    return jnp.concatenate([prev[tm:], cur], axis=0)


def _ada_kernel(c_ref, w_ref, b_ref, o_ref):
    part = _dot(_silu(c_ref[...]).astype(BF16), w_ref[...].astype(BF16))

    @pl.when(pl.program_id(0) == 0)
    def _():
        o_ref[...] = part + b_ref[...]

    @pl.when(pl.program_id(0) > 0)
    def _():
        o_ref[...] += part


def _ada(c_all, w_ada, b_ada, tk=256):
    rows, d = c_all.shape
    n = w_ada.shape[1]
    return pl.pallas_call(
        _ada_kernel,
        grid=(d // tk,),
        in_specs=[pl.BlockSpec((rows, tk), lambda k: (0, k)),
                  pl.BlockSpec((tk, n), lambda k: (k, 0)),
                  pl.BlockSpec((1, n), lambda k: (0, 0))],
        out_specs=pl.BlockSpec((rows, n), lambda k: (0, 0)),
        out_shape=jax.ShapeDtypeStruct((rows, n), F32),
        compiler_params=_cparams(1),
        name="ada",
    )(c_all, w_ada, b_ada.reshape(1, n))


def _inproj0_kernel(x_ref, shift_ref, scale_ref, state_ref, wq, wk, wv, wza, wbg, wcg, wxi, wzb, wconv_ref,
                    q_ref, k_ref, v_ref, za_ref, ob_ref, tail_ref, u_scr, carry_scr, *, stride, tiles_per_seq):
    i = pl.program_id(0)
    j = pl.program_id(1)

    @pl.when(j == 0)
    def _():
        u_scr[...] = _modulate(x_ref[...], scale_ref[...], shift_ref[...]).astype(BF16)

    @pl.when(i % tiles_per_seq == 0)
    def _():
        carry_scr[j] = state_ref[...]

    u = u_scr[...]
    q_ref[...] = (_dot(u, wq[...]) * Q_SCALE).astype(q_ref.dtype)
    k_ref[...] = _dot(u, wk[...])
    v_ref[...] = _dot(u, wv[...])
    za_ref[...] = _dot(u, wza[...])
    ci = _dot(u, wcg[...]) * _dot(u, wxi[...])
    prev = carry_scr[j]
    wc = wconv_ref[...]
    conv = wc[2:3] * ci + wc[1:2] * _shift_rows(ci, prev, stride) + wc[0:1] * _shift_rows(ci, prev, 2 * stride)
    ob_ref[...] = (_dot(u, wbg[...]) * conv * _silu(_dot(u, wzb[...]))).astype(ob_ref.dtype)
    new_prev = _next_prev(ci, prev)
    carry_scr[j] = new_prev
    tail_ref[...] = new_prev


def _inproj0(x, shift, scale, state, w_in, w_conv, *, tm, tn, stride, tiles_per_seq, q_dtype):
    rows, d = x.shape
    c = w_conv.shape[1]
    p = state.shape[1]
    n_i, n_j = rows // tm, c // tn
    gm = shift.shape[-2]
    if shift.ndim == 3:
        mod_spec = pl.BlockSpec((None, gm, d), lambda i, j: (i // tiles_per_seq, 0, 0))
    else:
        mod_spec = pl.BlockSpec((gm, d), lambda i, j: (0, 0))

    def w_spec(g):
        return pl.BlockSpec((d, tn), lambda i, j: (0, g * n_j + j))

    act_spec = pl.BlockSpec((tm, tn), lambda i, j: (i, j))
    outs = pl.pallas_call(
        functools.partial(_inproj0_kernel, stride=stride, tiles_per_seq=tiles_per_seq),
        grid=(n_i, n_j),
        in_specs=[pl.BlockSpec((tm, d), lambda i, j: (i, 0)), mod_spec, mod_spec,
                  pl.BlockSpec((None, p, tn), lambda i, j: (i // tiles_per_seq, 0, j))]
                 + [w_spec(g) for g in range(8)]
                 + [pl.BlockSpec((w_conv.shape[0], tn), lambda i, j: (0, j))],
        out_specs=[act_spec] * 5 + [pl.BlockSpec((None, p, tn), lambda i, j: (i, 0, j))],
        out_shape=[jax.ShapeDtypeStruct((rows, c), q_dtype)] + [jax.ShapeDtypeStruct((rows, c), F32)] * 3
                  + [jax.ShapeDtypeStruct((rows, c), BF16), jax.ShapeDtypeStruct((n_i, p, c), F32)],
        scratch_shapes=[pltpu.VMEM((tm, d), BF16), pltpu.VMEM((n_j, p, tn), F32)],
        compiler_params=_cparams(2),
        name="inproj0",
    )(x, shift, scale, state, *([w_in] * 8), w_conv)
    return outs


def _sb_softplus(z2, mask):
    neg_abs = lax.bitcast_convert_type(lax.bitcast_convert_type(z2, jnp.uint32) | jnp.uint32(0x80000000), F32)
    sp = jnp.maximum(z2, 0.0) + jnp.log2(1.0 + jnp.exp2(neg_abs))
    return sp if mask is None else jnp.where(mask, sp, 0.0)


def _sb_exp(z2, incl, carry, mask):
    w = jnp.exp2(z2 - incl - carry)
    return (w if mask is None else jnp.where(mask, w, 0.0)).astype(BF16)


def _tri(t):
    r = lax.broadcasted_iota(jnp.int32, (t, t), 0)
    c = lax.broadcasted_iota(jnp.int32, (t, t), 1)
    return jnp.where(r >= c, 1.0, 0.0).astype(BF16)


def _suffix_sum_heads(x, n_heads):
    n = x.shape[1]
    lane = lax.broadcasted_iota(jnp.int32, x.shape, 1)
    s = n_heads
    while s < n:
        x = x + jnp.where(lane < n - s, pltpu.roll(x, n - s, 1), 0.0)
        s *= 2
    return x


def _bcast_heads(v, n_heads):
    lane = lax.broadcasted_iota(jnp.int32, v.shape, 1)
    v = jnp.where(lane < n_heads, v, 0.0)
    s = n_heads
    while s < v.shape[1]:
        v = v + pltpu.roll(v, s, 1)
        s *= 2
    return v


class _SampleAttn:
    def __init__(self, bias_ref, q_ref, kn_ref, vn_ref, k_pages, v_pages, n_heads):
        self.n_heads = n_heads
        self.t_new = t_new = q_ref.shape[0]
        self.k_pages, self.v_pages, self.kn_ref, self.vn_ref = k_pages, v_pages, kn_ref, vn_ref
        n_pages = len(k_pages)
        rows_pp = k_pages[0].shape[0]
        self.blocks = [n_pages] + list(range(n_pages - 1, -1, -1))
        self.sizes = [HEAD_DIM if j == n_pages else rows_pp for j in self.blocks]
        q = q_ref[...]
        self.q_rows = jnp.concatenate([q[:, h * HEAD_DIM:(h + 1) * HEAD_DIM] for h in range(n_heads)],
                                      axis=0).astype(BF16)
        row = lax.broadcasted_iota(jnp.int32, (t_new, HEAD_DIM), 0)
        lanes = {n: lax.broadcasted_iota(jnp.int32, (t_new, n), 1) for n in set(self.sizes)}
        self.heads_of = {n: _imod(lanes[n], n_heads) for n in set(self.sizes)}
        self.biases = {}
        for n in set(self.sizes):
            bias = jnp.full((1, n), bias_ref[0, 0] * LOG2E, F32)
            for h in range(1, n_heads):
                bias = jnp.where(self.heads_of[n][0:1] == h, bias_ref[0, h] * LOG2E, bias)
            self.biases[n] = bias
        new_mask = _idiv(lanes[HEAD_DIM], n_heads) < row
        self.masks = [new_mask if j == n_pages else None for j in self.blocks]

    def _block(self, pages, new_ref, j):
        if j < len(pages):
            return pages[j][...].astype(BF16)
        pad = jnp.zeros((HEAD_DIM - new_ref.shape[0], HEAD_DIM), F32)
        return jnp.concatenate([new_ref[...], pad], axis=0).astype(BF16)

    def logits(self):
        t_new = self.t_new
        self.z2s = []
        for j, n in zip(self.blocks, self.sizes):
            s = _dot_nt(self.q_rows, self._block(self.k_pages, self.kn_ref, j))
            z2 = s[0:t_new]
            for h in range(1, self.n_heads):
                z2 = jnp.where(self.heads_of[n] == h, s[h * t_new:(h + 1) * t_new], z2)
            self.z2s.append(z2 + self.biases[n])

    def weights(self):
        sps = [_sb_softplus(z2, m) for z2, m in zip(self.z2s, self.masks)]
        incls = [_suffix_sum_heads(sp, self.n_heads) for sp in sps]
        totals = [_bcast_heads(incl[:, 0:HEAD_DIM], self.n_heads) for incl in incls]
        carry = jnp.zeros((self.t_new, HEAD_DIM), F32)
        self.w_rows = []
        for i, n in enumerate(self.sizes):
            w = jnp.exp2(self.z2s[i] - incls[i] - jnp.concatenate([carry] * (n // HEAD_DIM), axis=1))
            if self.masks[i] is not None:
                w = jnp.where(self.masks[i], w, 0.0)
            self.w_rows.append(jnp.concatenate([jnp.where(self.heads_of[n] == h, w, 0.0)
                                                for h in range(self.n_heads)], axis=0).astype(BF16))
            carry = carry + totals[i]

    def output(self, za_ref, o_ref):
        t_new = self.t_new
        acc = jnp.zeros((self.n_heads * t_new, HEAD_DIM), F32)
        for j, wr in zip(self.blocks, self.w_rows):
            acc = acc + _dot(wr, self._block(self.v_pages, self.vn_ref, j))
        o = jnp.concatenate([acc[h * t_new:(h + 1) * t_new] for h in range(self.n_heads)], axis=1)
        o_ref[...] = (o * _silu(za_ref[...])).astype(o_ref.dtype)


def _pick(cond, a, b):
    if isinstance(cond, bool):
        return a if cond else b
    return jnp.where(cond, a, b)


class _PromptAttn:
    def __init__(self, bias, q_ref, k_ref, v_ref, s, t):
        self.q_ref, self.k_ref, self.v_ref, self.s, self.t = q_ref, k_ref, v_ref, s, t
        self.nq = nq = q_ref.shape[0] // t
        self.n_items = nq + 1
        self.bias = bias * LOG2E
        self.in_a = [True if i == 0 else False if i >= nq // 2 else i <= s for i in range(self.n_items)]
        self.qrow = [pl.multiple_of(_pick(self.in_a[i], s, nq - 1 - s) * t, t) for i in range(self.n_items)]
        self.krow = [pl.multiple_of(_pick(self.in_a[i], s - i, nq - i) * t, t) for i in range(self.n_items)]
        r = lax.broadcasted_iota(jnp.int32, (t, t), 0)
        c = lax.broadcasted_iota(jnp.int32, (t, t), 1)
        self.masks = []
        for i in range(self.n_items):
            if i == 0:
                self.masks.append(c < r)
            elif i <= nq // 2:
                self.masks.append(c < r + jnp.where(i == s + 1, 0, t))
            else:
                self.masks.append(None)

    def logits(self):
        t = self.t
        self.zs = [_dot_nt(self.q_ref[pl.ds(self.qrow[i], t), :], self.k_ref[pl.ds(self.krow[i], t), :].astype(BF16))
                   + self.bias for i in range(self.n_items)]

    def weights(self):
        t = self.t
        tri = _tri(t)
        sps = [_sb_softplus(z2, m) for z2, m in zip(self.zs, self.masks)]
        incls = [_dot(sp.astype(BF16), tri) for sp in sps]
        run_a = jnp.zeros((t, 1), F32)
        run_b = jnp.zeros((t, 1), F32)
        self.ws = []
        for i in range(self.n_items):
            rowsum = jnp.sum(sps[i], axis=-1, keepdims=True)
            carry = _pick(self.in_a[i], run_a, run_b)
            self.ws.append(_sb_exp(self.zs[i], incls[i], carry, self.masks[i]))
            run_a = run_a + _pick(self.in_a[i], rowsum, 0.0)
            run_b = run_b + _pick(self.in_a[i], 0.0, rowsum)

    def output(self, za_ref, o_ref):
        t, s = self.t, self.s
        acc_a = jnp.zeros((t, HEAD_DIM), F32)
        acc_b = jnp.zeros((t, HEAD_DIM), F32)
        for i in range(self.n_items):
            pv = _dot(self.ws[i], self.v_ref[pl.ds(self.krow[i], t), :].astype(BF16))
            acc_a = acc_a + _pick(self.in_a[i], pv, 0.0)
            acc_b = acc_b + _pick(self.in_a[i], 0.0, pv)
        for acc, tile in ((acc_a, s), (acc_b, self.nq - 1 - s)):
            rows = pl.ds(pl.multiple_of(tile * t, t), t)
            o_ref[rows, :] = (acc * _silu(za_ref[rows, :])).astype(o_ref.dtype)


def _attn_kernel(pt_ref, bias_ref, q_ref, kn_ref, vn_ref, za_ref, *rest, n_pages, n_heads, t):
    del pt_ref
    k_pages = rest[:n_pages]
    v_pages = rest[n_pages:2 * n_pages]
    qp_ref, kp_ref, vp_ref, zap_ref, o_ref, op_ref = rest[2 * n_pages:]
    nq = qp_ref.shape[0] // t
    n = pl.program_id(0)
    pairs = nq // 2
    head = _imod(_idiv(n, pairs), n_heads)
    sample = _SampleAttn(bias_ref, q_ref, kn_ref, vn_ref, k_pages, v_pages, n_heads)
    prompt = _PromptAttn(bias_ref[0, head], qp_ref, kp_ref, vp_ref, _imod(n, pairs), t)
    sample.logits()
    prompt.logits()
    sample.weights()
    prompt.weights()
    sample.output(za_ref, o_ref)
    prompt.output(zap_ref, op_ref)


def _attn(page_table, bias, q, k_new, v_new, za, cache_k, cache_v, qp, kp, vp, zap, *, n_phys, n_batch, seq):
    n_seq, t_new, width = q.shape
    n_pages = page_table.shape[1]
    n_heads = width // HEAD_DIM
    page = cache_k.shape[0] // (n_phys * n_heads)
    t = SB_BLOCK
    pairs = seq // t // 2
    assert n_seq == n_batch * n_heads * pairs, "one (prompt sequence, head, query-tile pair) per sample sequence"
    seq_spec = pl.BlockSpec((None, t_new, width), lambda n, pt: (n, 0, 0))
    new_spec = pl.BlockSpec((None, t_new * n_heads, HEAD_DIM), lambda n, pt: (n, 0, 0))
    head_spec = pl.BlockSpec((seq, HEAD_DIM), lambda n, pt: (n // (n_heads * pairs), (n // pairs) % n_heads))

    def page_spec(p):
        return pl.BlockSpec((page * n_heads, HEAD_DIM), lambda n, pt: (pt[n, p], 0))

    grid_spec = pltpu.PrefetchScalarGridSpec(
        num_scalar_prefetch=1,
        grid=(n_seq,),
        in_specs=[pl.BlockSpec(memory_space=pltpu.SMEM), seq_spec, new_spec, new_spec, seq_spec]
                 + [page_spec(p) for p in range(n_pages)] * 2 + [head_spec] * 4,
        out_specs=[seq_spec, head_spec],
    )
    return pl.pallas_call(
        functools.partial(_attn_kernel, n_pages=n_pages, n_heads=n_heads, t=t),
        grid_spec=grid_spec,
        out_shape=[jax.ShapeDtypeStruct((n_seq, t_new, width), F32), jax.ShapeDtypeStruct(qp.shape, BF16)],
        compiler_params=_cparams(1),
        name="attn",
    )(page_table, bias, q, k_new, v_new, za, *([cache_k] * n_pages), *([cache_v] * n_pages), qp, kp, vp, zap)


def _residual_ln(x, mix, gate, g, b, alpha):
    r = alpha * x + _gate_mix(mix, gate)
    mu = jnp.mean(r, axis=-1, keepdims=True)
    rc = r - mu
    var = jnp.mean(rc * rc, axis=-1, keepdims=True)
    return rc * lax.rsqrt(var + LN_EPS) * g + b


def _outproj_ln_kernel(*refs, n_in, alpha):
    x_ref, gate_ref = refs[0], refs[1]
    acts = refs[2:2 + n_in]
    ws = refs[2 + n_in:2 + 2 * n_in]
    g_ref, b_ref, y_ref = refs[2 + 2 * n_in:]
    mix = _dot(acts[0][...].astype(BF16), ws[0][...])
    for a, w in zip(acts[1:], ws[1:]):
        mix = mix + _dot(a[...].astype(BF16), w[...])
    y_ref[...] = _residual_ln(x_ref[...], mix, gate_ref[...], g_ref[...], b_ref[...], alpha)


def _outproj_ln(x, gate, acts, w_out, ln_g, ln_b, *, tm, tiles_per_seq, alpha):
    rows, d = x.shape
    gm = gate.shape[-2]
    if gate.ndim == 3:
        mod_spec = pl.BlockSpec((None, gm, d), lambda i: (i // tiles_per_seq, 0, 0))
    else:
        mod_spec = pl.BlockSpec((gm, d), lambda i: (0, 0))
    n_in = len(acts)
    kw = w_out.shape[0] // n_in
    once = pl.Buffered(1)
    w_specs = [pl.BlockSpec((kw, d), functools.partial(lambda i, g: (g, 0), g=g), pipeline_mode=once)
               for g in range(n_in)]
    vec_spec = pl.BlockSpec((1, d), lambda i: (0, 0))
    return pl.pallas_call(
        functools.partial(_outproj_ln_kernel, n_in=n_in, alpha=alpha),
        grid=(rows // tm,),
        in_specs=[pl.BlockSpec((tm, d), lambda i: (i, 0)), mod_spec]
                 + [pl.BlockSpec((tm, kw), lambda i: (i, 0))] * n_in + w_specs + [vec_spec, vec_spec],
        out_specs=pl.BlockSpec((tm, d), lambda i: (i, 0)),
        out_shape=jax.ShapeDtypeStruct((rows, d), F32),
        compiler_params=_cparams(1),
        name="outproj_ln",
    )(x, gate, *acts, *([w_out] * n_in), ln_g.reshape(1, d), ln_b.reshape(1, d))


def _scan_rows(a, b, h_scr, h0):
    rows, d = a.shape
    g = rows // SUBLANES
    a = a.reshape(g, SUBLANES, d)
    b = b.reshape(g, SUBLANES, d)
    row = lax.broadcasted_iota(jnp.int32, (1, SUBLANES, d), 1)
    s = 1
    while s < SUBLANES:
        keep = row >= s
        b = a * jnp.where(keep, pltpu.roll(b, s, 1), 0.0) + b
        a = a * jnp.where(keep, pltpu.roll(a, s, 1), 1.0)
        s *= 2
    h = h0
    for i in range(g):
        hg = a[i] * h + b[i]
        h_scr[i * SUBLANES:(i + 1) * SUBLANES, :] = hg
        h = hg[SUBLANES - 1:]
    return h


def _lru_kernel(x_ref, shift_ref, scale_ref, gate_ref, state_ref, h0_ref, wx_ref, wz_ref, wconv_ref, bconv_ref,
                wga_ref, wgx_ref, bga_ref, bgx_ref, lam_ref, wout_ref, lng_ref, lnb_ref,
                y_ref, tail_ref, hlast_ref, prev_scr, hcar_scr, h_scr, *, stride, tiles_per_seq, alpha):
    i = pl.program_id(0)

    @pl.when(i % tiles_per_seq == 0)
    def _():
        prev_scr[...] = state_ref[...]
        hcar_scr[...] = h0_ref[...]

    u = _modulate(x_ref[...], scale_ref[...], shift_ref[...]).astype(BF16)
    xr = _dot(u, wx_ref[...])
    prev = prev_scr[...]
    wc = wconv_ref[...]
    xc = wc[3:4] * xr + bconv_ref[...]
    for kk in range(1, 4):
        xc = xc + wc[3 - kk:4 - kk] * _shift_rows(xr, prev, kk * stride)
    new_prev = _next_prev(xr, prev)
    prev_scr[...] = new_prev
    tail_ref[...] = new_prev

    xcb = xc.astype(BF16)
    n_blocks, blk, _ = wga_ref.shape
    ga = jnp.concatenate([_dot(xcb[:, n * blk:(n + 1) * blk], wga_ref[n]) for n in range(n_blocks)], axis=1)
    gx = jnp.concatenate([_dot(xcb[:, n * blk:(n + 1) * blk], wgx_ref[n]) for n in range(n_blocks)], axis=1)
    lam = lam_ref[...]
    neg_log_sig = jnp.maximum(-lam, 0.0) + jnp.log1p(jnp.exp(-jnp.abs(lam)))
    log_a = (-LRU_C) * jax.nn.sigmoid(ga + bga_ref[...]) * neg_log_sig
    a = jnp.exp(log_a)
    bt = jnp.sqrt(jnp.tanh(-log_a) * (1.0 + a * a)) * jax.nn.sigmoid(gx + bgx_ref[...]) * xc

    if stride == x_ref.shape[0]:
        h = a * hcar_scr[...] + bt
        hcar_scr[...] = h
    else:
        hcar_scr[...] = _scan_rows(a, bt, h_scr, hcar_scr[...])
        h = h_scr[...]
    hlast_ref[...] = hcar_scr[...]
    z = _dot(u, wz_ref[...])
    mix = _dot((h * _silu(z)).astype(BF16), wout_ref[...])
    y_ref[...] = _residual_ln(x_ref[...], mix, gate_ref[...], lng_ref[...], lnb_ref[...], alpha)


def _lru(x, shift, scale, gate, state, h0, w_in, w_conv, b_conv, w_ga, w_gx, b_ga, b_gx, lam, w_out, ln_g, ln_b, *,
         tm, stride, tiles_per_seq, alpha):
    rows, d = x.shape
    p = state.shape[1]
    hr = h0.shape[1]
    n_i = rows // tm
    gm = shift.shape[-2]
    if shift.ndim == 3:
        mod_spec = pl.BlockSpec((None, gm, d), lambda i: (i // tiles_per_seq, 0, 0))
    else:
        mod_spec = pl.BlockSpec((gm, d), lambda i: (0, 0))
    once = pl.Buffered(1)
    vec_spec = pl.BlockSpec((1, d), lambda i: (0, 0))
    gate_spec = pl.BlockSpec(w_ga.shape, lambda i: (0, 0, 0), pipeline_mode=once)
    scan_scr = (tm, d) if stride != tm else (SUBLANES, 128)
    return pl.pallas_call(
        functools.partial(_lru_kernel, stride=stride, tiles_per_seq=tiles_per_seq, alpha=alpha),
        grid=(n_i,),
        in_specs=[pl.BlockSpec((tm, d), lambda i: (i, 0)), mod_spec, mod_spec, mod_spec,
                  pl.BlockSpec((None, p, d), lambda i: (i // tiles_per_seq, 0, 0)),
                  pl.BlockSpec((None, hr, d), lambda i: (i // tiles_per_seq, 0, 0)),
                  pl.BlockSpec((d, d), lambda i: (0, 0), pipeline_mode=once),
                  pl.BlockSpec((d, d), lambda i: (0, 1), pipeline_mode=once),
                  pl.BlockSpec((w_conv.shape[0], d), lambda i: (0, 0)), vec_spec,
                  gate_spec, gate_spec, vec_spec, vec_spec, vec_spec,
                  pl.BlockSpec((d, d), lambda i: (0, 0), pipeline_mode=once), vec_spec, vec_spec],
        out_specs=[pl.BlockSpec((tm, d), lambda i: (i, 0)),
                   pl.BlockSpec((None, p, d), lambda i: (i, 0, 0)),
                   pl.BlockSpec((None, hr, d), lambda i: (i, 0, 0))],
        out_shape=[jax.ShapeDtypeStruct((rows, d), F32), jax.ShapeDtypeStruct((n_i, p, d), F32),
                   jax.ShapeDtypeStruct((n_i, hr, d), F32)],
        scratch_shapes=[pltpu.VMEM((p, d), F32), pltpu.VMEM((hr, d), F32), pltpu.VMEM(scan_scr, F32)],
        compiler_params=_cparams(1),
        name="lru",
    )(x, shift, scale, gate, state, h0, w_in, w_in, w_conv, b_conv.reshape(1, d), w_ga, w_gx,
      b_ga.reshape(1, d), b_gx.reshape(1, d), lam.reshape(1, d), w_out, ln_g.reshape(1, d), ln_b.reshape(1, d))


def kernel(x_prompt, x_sample, c_prompt, c_sample, cache_k, cache_v, state_conv_b, state_conv_c, state_h, page_table, we_ada, be_ada, we_in, we_sb_bias, we_conv, we_out, ge_ln, be_ln, wo_ada, bo_ada, wo_in, wo_conv, bo_conv, wo_gate_a, bo_gate_a, wo_gate_x, bo_gate_x, wo_lambda, wo_out, go_ln, bo_ln):
    nb, seq, d = x_prompt.shape
    ns, dec, _ = x_sample.shape
    assert we_ada.shape[0] == 1 and wo_ada.shape[0] == 1, "one even and one odd layer"
    n_heads = we_sb_bias.shape[1]
    width = n_heads * HEAD_DIM
    alpha = (2.0 * (we_ada.shape[0] + wo_ada.shape[0])) ** 0.25
    tm = 512
    tm_in0 = 1024

    def to_time_major(a):
        return jnp.swapaxes(a, 0, 1).reshape(a.shape[1] * ns, a.shape[2])

    def to_seq_major(a, t):
        return jnp.swapaxes(a.reshape(t, ns, a.shape[-1]), 0, 1)

    pad = (-(nb + ns)) % SUBLANES
    c_all = jnp.concatenate([c_prompt, c_sample, jnp.zeros((pad, d), F32)], axis=0)
    mods = []
    for w_ada, b_ada in ((we_ada[0], be_ada[0]), (wo_ada[0], bo_ada[0])):
        m = _ada(c_all, w_ada, b_ada)
        parts = [m[:, g * d:(g + 1) * d] for g in range(3)]
        mods.append(([v[:nb].reshape(nb, 1, d) for v in parts], [v[nb:nb + ns] for v in parts]))

    xp = x_prompt.reshape(nb * seq, d)
    xs = to_time_major(x_sample)
    bias = we_sb_bias[0].reshape(1, n_heads)

    (shift_p, scale_p, gate_p), (shift_s, scale_s, gate_s) = mods[0]
    w_in0 = we_in[0].astype(BF16)
    w_out0 = we_out[0].astype(BF16)
    conv_k = we_conv.shape[1]
    qp, kp, vp, zap, obp, tailp = _inproj0(
        xp, shift_p, scale_p, jnp.zeros((nb, SUBLANES, width), F32), w_in0, we_conv[0],
        tm=tm_in0, tn=256, stride=1, tiles_per_seq=seq // tm_in0, q_dtype=BF16)
    conv_b_prompt = tailp.reshape(nb, seq // tm_in0, SUBLANES, width)[:, -1, SUBLANES - (conv_k - 1):][None]

    state_b = to_time_major(state_conv_b[0])[None]
    rows_s = ns * dec
    tiles_s = rows_s // tm
    qs, ks, vs, zas, obs, tails = _inproj0(
        xs, shift_s, scale_s, state_b, w_in0, we_conv[0],
        tm=tm, tn=256, stride=ns, tiles_per_seq=tiles_s, q_dtype=F32)
    k_sample = to_seq_major(ks, dec).reshape(1, ns, dec, n_heads, HEAD_DIM)
    v_sample = to_seq_major(vs, dec).reshape(1, ns, dec, n_heads, HEAD_DIM)
    n_phys = cache_k.shape[1]
    oas, oap = _attn(page_table, bias, to_seq_major(qs, dec), k_sample.reshape(ns, dec * n_heads, HEAD_DIM),
                     v_sample.reshape(ns, dec * n_heads, HEAD_DIM), to_seq_major(zas, dec),
                     cache_k.reshape(-1, HEAD_DIM), cache_v.reshape(-1, HEAD_DIM), qp, kp, vp, zap,
                     n_phys=n_phys, n_batch=nb, seq=seq)
    yp = _outproj_ln(xp, gate_p, [oap, obp], w_out0, ge_ln[0], be_ln[0], tm=tm, tiles_per_seq=seq // tm,
                     alpha=alpha)
    ys = _outproj_ln(xs, gate_s, [to_time_major(oas), obs], w_out0, ge_ln[0], be_ln[0], tm=tm,
                     tiles_per_seq=tiles_s, alpha=alpha)
    conv_b_sample = to_seq_major(tails[-1], conv_k - 1)[None]

    (shift_p, scale_p, gate_p), (shift_s, scale_s, gate_s) = mods[1]
    w_in1 = wo_in[0].astype(BF16)
    w_out1 = wo_out[0].astype(BF16)
    w_ga = wo_gate_a[0].astype(BF16)
    w_gx = wo_gate_x[0].astype(BF16)
    conv_k = wo_conv.shape[1]
    lru_w = (w_in1, wo_conv[0], bo_conv[0], w_ga, w_gx, bo_gate_a[0], bo_gate_x[0], wo_lambda[0],
             w_out1, go_ln[0], bo_ln[0])
    tm1 = 256
    yp, tailp, hp = _lru(yp, shift_p, scale_p, gate_p, jnp.zeros((nb, SUBLANES, d), F32),
                         jnp.zeros((nb, 1, d), F32), *lru_w, tm=tm1, stride=1, tiles_per_seq=seq // tm1, alpha=alpha)
    conv_c_prompt = tailp.reshape(nb, seq // tm1, SUBLANES, d)[:, -1, SUBLANES - (conv_k - 1):][None]
    h_prompt = hp.reshape(nb, seq // tm1, d)[:, -1][None]

    state_c = to_time_major(state_conv_c[0])[None]
    ys, tails, hs = _lru(ys, shift_s, scale_s, gate_s, state_c, state_h[0][None], *lru_w, tm=ns, stride=ns,
                         tiles_per_seq=dec, alpha=alpha)
    conv_c_sample = to_seq_major(tails[-1], conv_k - 1)[None]
    h_sample = hs[-1][None]

    y_prompt = yp.reshape(nb, seq, d)
    y_sample = to_seq_major(ys, dec)
    k_prompt = kp.reshape(1, nb, seq, n_heads, HEAD_DIM)
    v_prompt = vp.reshape(1, nb, seq, n_heads, HEAD_DIM)
    return (y_prompt, y_sample, k_prompt, v_prompt, conv_b_prompt, conv_c_prompt, h_prompt,
            k_sample, v_sample, conv_b_sample, conv_c_sample, h_sample)
```

```python
import functools
import math

import jax
import jax.numpy as jnp
from jax import lax
from jax.experimental import pallas as pl
from jax.experimental.pallas import tpu as pltpu

F32 = jnp.float32
BF16 = jnp.bfloat16

HEAD_DIM = 128
SB_BLOCK = 256
LRU_C = 8.0
LN_EPS = 1e-5
LOG2E = math.log2(math.e)
Q_SCALE = LOG2E / math.sqrt(HEAD_DIM)
SUBLANES = 8
VMEM_LIMIT = 58 * 1024 * 1024


def _cparams(n_axes):
    return pltpu.CompilerParams(dimension_semantics=("arbitrary",) * n_axes, vmem_limit_bytes=VMEM_LIMIT)


def _silu(x):
    return x * jax.nn.sigmoid(x)


def _idiv(x, n):
    assert n & (n - 1) == 0
    return x >> (n.bit_length() - 1)


def _imod(x, n):
    assert n & (n - 1) == 0
    return x & (n - 1)


def _dot(a, b):
    return jnp.dot(a, b, preferred_element_type=F32)


def _dot_nt(a, b):
    return lax.dot_general(a, b, (((1,), (1,)), ((), ())), preferred_element_type=F32)


def _modulate(x, scale, shift):
    gm = scale.shape[0]
    if gm == 1:
        return x * (1.0 + scale) + shift
    rows, d = x.shape
    x3 = x.reshape(rows // gm, gm, d)
    return (x3 * (1.0 + scale)[None] + shift[None]).reshape(rows, d)


def _gate_mix(mix, gate):
    gm = gate.shape[0]
    if gm == 1:
        return mix * (1.0 + gate)
    rows, d = mix.shape
    return (mix.reshape(rows // gm, gm, d) * (1.0 + gate)[None]).reshape(rows, d)


def _shift_rows(cur, prev, d):
    tm, p = cur.shape[0], prev.shape[0]
    if d % SUBLANES == 0:
        if d >= tm:
            return prev[p - d:p - d + tm]
        return jnp.concatenate([prev[p - d:], cur[:tm - d]], axis=0)
    assert d < SUBLANES <= p
    rolled = pltpu.roll(cur, d, 0)
    prev_rolled = pltpu.roll(prev[p - SUBLANES:], d, 0)
    row = lax.broadcasted_iota(jnp.int32, prev_rolled.shape, 0)
    head = jnp.where(row < d, prev_rolled, rolled[:SUBLANES])
    return jnp.concatenate([head, rolled[SUBLANES:]], axis=0)


def _next_prev(cur, prev):
    tm, p = cur.shape[0], prev.shape[0]
    if tm >= p:
        return cur[tm - p:]
    return jnp.concatenate([prev[tm:], cur], axis=0)


def _ada_kernel(c_ref, w_ref, b_ref, o_ref):
    part = _dot(_silu(c_ref[...]).astype(BF16), w_ref[...].astype(BF16))

    @pl.when(pl.program_id(0) == 0)
    def _():
        o_ref[...] = part + b_ref[...]

    @pl.when(pl.program_id(0) > 0)
    def _():
        o_ref[...] += part


def _ada(c_all, w_ada, b_ada, tk=256):
    rows, d = c_all.shape
    n = w_ada.shape[1]
    return pl.pallas_call(
        _ada_kernel,
        grid=(d // tk,),
        in_specs=[pl.BlockSpec((rows, tk), lambda k: (0, k)),
                  pl.BlockSpec((tk, n), lambda k: (k, 0)),
                  pl.BlockSpec((1, n), lambda k: (0, 0))],
        out_specs=pl.BlockSpec((rows, n), lambda k: (0, 0)),
        out_shape=jax.ShapeDtypeStruct((rows, n), F32),
        compiler_params=_cparams(1),
        name="ada",
    )(c_all, w_ada, b_ada.reshape(1, n))


def _inproj0_kernel(x_ref, shift_ref, scale_ref, state_ref, wq, wk, wv, wza, wbg, wcg, wxi, wzb, wconv_ref,
                    q_ref, k_ref, v_ref, za_ref, ob_ref, tail_ref, u_scr, carry_scr, *, stride, tiles_per_seq):
    i = pl.program_id(0)
    j = pl.program_id(1)

    @pl.when(j == 0)
    def _():
        u_scr[...] = _modulate(x_ref[...], scale_ref[...], shift_ref[...]).astype(BF16)

    @pl.when(i % tiles_per_seq == 0)
    def _():
        carry_scr[j] = state_ref[...]

    u = u_scr[...]
    q_ref[...] = (_dot(u, wq[...]) * Q_SCALE).astype(q_ref.dtype)
    k_ref[...] = _dot(u, wk[...])
    v_ref[...] = _dot(u, wv[...])
    za_ref[...] = _dot(u, wza[...])
    ci = _dot(u, wcg[...]) * _dot(u, wxi[...])
    prev = carry_scr[j]
    wc = wconv_ref[...]
    conv = wc[2:3] * ci + wc[1:2] * _shift_rows(ci, prev, stride) + wc[0:1] * _shift_rows(ci, prev, 2 * stride)
    ob_ref[...] = (_dot(u, wbg[...]) * conv * _silu(_dot(u, wzb[...]))).astype(ob_ref.dtype)
    new_prev = _next_prev(ci, prev)
    carry_scr[j] = new_prev
    tail_ref[...] = new_prev


def _inproj0(x, shift, scale, state, w_in, w_conv, *, tm, tn, stride, tiles_per_seq, q_dtype):
    rows, d = x.shape
    c = w_conv.shape[1]
    p = state.shape[1]
    n_i, n_j = rows // tm, c // tn
    gm = shift.shape[-2]
    if shift.ndim == 3:
        mod_spec = pl.BlockSpec((None, gm, d), lambda i, j: (i // tiles_per_seq, 0, 0))
    else:
        mod_spec = pl.BlockSpec((gm, d), lambda i, j: (0, 0))

    def w_spec(g):
        return pl.BlockSpec((d, tn), lambda i, j: (0, g * n_j + j))

    act_spec = pl.BlockSpec((tm, tn), lambda i, j: (i, j))
    outs = pl.pallas_call(
        functools.partial(_inproj0_kernel, stride=stride, tiles_per_seq=tiles_per_seq),
        grid=(n_i, n_j),
        in_specs=[pl.BlockSpec((tm, d), lambda i, j: (i, 0)), mod_spec, mod_spec,
                  pl.BlockSpec((None, p, tn), lambda i, j: (i // tiles_per_seq, 0, j))]
                 + [w_spec(g) for g in range(8)]
                 + [pl.BlockSpec((w_conv.shape[0], tn), lambda i, j: (0, j))],
        out_specs=[act_spec] * 5 + [pl.BlockSpec((None, p, tn), lambda i, j: (i, 0, j))],
        out_shape=[jax.ShapeDtypeStruct((rows, c), q_dtype)] + [jax.ShapeDtypeStruct((rows, c), F32)] * 3
                  + [jax.ShapeDtypeStruct((rows, c), BF16), jax.ShapeDtypeStruct((n_i, p, c), F32)],
        scratch_shapes=[pltpu.VMEM((tm, d), BF16), pltpu.VMEM((n_j, p, tn), F32)],
        compiler_params=_cparams(2),
        name="inproj0",
    )(x, shift, scale, state, *([w_in] * 8), w_conv)
    return outs


def _sb_softplus(z2, mask):
    neg_abs = lax.bitcast_convert_type(lax.bitcast_convert_type(z2, jnp.uint32) | jnp.uint32(0x80000000), F32)
    sp = jnp.maximum(z2, 0.0) + jnp.log2(1.0 + jnp.exp2(neg_abs))
    return sp if mask is None else jnp.where(mask, sp, 0.0)


def _sb_exp(z2, incl, carry, mask):
    w = jnp.exp2(z2 - incl - carry)
    return (w if mask is None else jnp.where(mask, w, 0.0)).astype(BF16)


def _tri_strict(t):
    r = lax.broadcasted_iota(jnp.int32, (t, t), 0)
    c = lax.broadcasted_iota(jnp.int32, (t, t), 1)
    return jnp.where(r > c, 1.0, 0.0).astype(BF16)


def _suffix_sum_heads(x, n_heads):
    n = x.shape[1]
    lane = lax.broadcasted_iota(jnp.int32, x.shape, 1)
    s = n_heads
    while s < n:
        x = x + jnp.where(lane < n - s, pltpu.roll(x, n - s, 1), 0.0)
        s *= 2
    return x


def _bcast_heads(v, n_heads):
    lane = lax.broadcasted_iota(jnp.int32, v.shape, 1)
    v = jnp.where(lane < n_heads, v, 0.0)
    s = n_heads
    while s < v.shape[1]:
        v = v + pltpu.roll(v, s, 1)
        s *= 2
    return v


class _SampleAttn:
    def __init__(self, bias_ref, q_ref, kn_ref, vn_ref, k_pages, v_pages, n_heads):
        self.n_heads = n_heads
        self.t_new = t_new = q_ref.shape[0]
        self.k_pages, self.v_pages, self.kn_ref, self.vn_ref = k_pages, v_pages, kn_ref, vn_ref
        n_pages = len(k_pages)
        rows_pp = k_pages[0].shape[0]
        self.blocks = [n_pages] + list(range(n_pages - 1, -1, -1))
        self.sizes = [HEAD_DIM if j == n_pages else rows_pp for j in self.blocks]
        q = q_ref[...]
        self.q_rows = jnp.concatenate([q[:, h * HEAD_DIM:(h + 1) * HEAD_DIM] for h in range(n_heads)],
                                      axis=0).astype(BF16)
        row = lax.broadcasted_iota(jnp.int32, (t_new, HEAD_DIM), 0)
        lanes = {n: lax.broadcasted_iota(jnp.int32, (t_new, n), 1) for n in set(self.sizes)}
        self.heads_of = {n: _imod(lanes[n], n_heads) for n in set(self.sizes)}
        self.biases = {}
        for n in set(self.sizes):
            bias = jnp.full((1, n), bias_ref[0, 0] * LOG2E, F32)
            for h in range(1, n_heads):
                bias = jnp.where(self.heads_of[n][0:1] == h, bias_ref[0, h] * LOG2E, bias)
            self.biases[n] = bias
        new_mask = _idiv(lanes[HEAD_DIM], n_heads) < row
        self.masks = [new_mask if j == n_pages else None for j in self.blocks]

    def _block(self, pages, new_ref, j):
        if j < len(pages):
            return pages[j][...].astype(BF16)
        pad = jnp.zeros((HEAD_DIM - new_ref.shape[0], HEAD_DIM), F32)
        return jnp.concatenate([new_ref[...], pad], axis=0).astype(BF16)

    def logits(self):
        t_new = self.t_new
        self.z2s = []
        for j, n in zip(self.blocks, self.sizes):
            s = _dot_nt(self.q_rows, self._block(self.k_pages, self.kn_ref, j))
            z2 = s[0:t_new]
            for h in range(1, self.n_heads):
                z2 = jnp.where(self.heads_of[n] == h, s[h * t_new:(h + 1) * t_new], z2)
            self.z2s.append(z2 + self.biases[n])

    def weights(self):
        sps = [_sb_softplus(z2, m) for z2, m in zip(self.z2s, self.masks)]
        incls = [_suffix_sum_heads(sp, self.n_heads) for sp in sps]
        totals = [_bcast_heads(incl[:, 0:HEAD_DIM], self.n_heads) for incl in incls]
        carry = jnp.zeros((self.t_new, HEAD_DIM), F32)
        self.w_rows = []
        for i, n in enumerate(self.sizes):
            w = jnp.exp2(self.z2s[i] - incls[i] - jnp.concatenate([carry] * (n // HEAD_DIM), axis=1))
            if self.masks[i] is not None:
                w = jnp.where(self.masks[i], w, 0.0)
            self.w_rows.append(jnp.concatenate([jnp.where(self.heads_of[n] == h, w, 0.0)
                                                for h in range(self.n_heads)], axis=0).astype(BF16))
            carry = carry + totals[i]

    def output(self, za_ref, o_ref):
        t_new = self.t_new
        acc = jnp.zeros((self.n_heads * t_new, HEAD_DIM), F32)
        for j, wr in zip(self.blocks, self.w_rows):
            acc = acc + _dot(wr, self._block(self.v_pages, self.vn_ref, j))
        o = jnp.concatenate([acc[h * t_new:(h + 1) * t_new] for h in range(self.n_heads)], axis=1)
        o_ref[...] = (o * _silu(za_ref[...])).astype(o_ref.dtype)


def _pick(cond, a, b):
    if isinstance(cond, bool):
        return a if cond else b
    return jnp.where(cond, a, b)


class _PromptAttn:
    def __init__(self, bias, q_ref, k_ref, v_ref, s, t):
        self.q_ref, self.k_ref, self.v_ref, self.s, self.t = q_ref, k_ref, v_ref, s, t
        self.nq = nq = q_ref.shape[0] // t
        self.n_items = nq + 1
        self.bias = bias * LOG2E
        self.in_a = [True if i == 0 else False if i >= nq // 2 else i <= s for i in range(self.n_items)]
        self.qrow = [pl.multiple_of(_pick(self.in_a[i], s, nq - 1 - s) * t, t) for i in range(self.n_items)]
        self.krow = [pl.multiple_of(_pick(self.in_a[i], s - i, nq - i) * t, t) for i in range(self.n_items)]
        r = lax.broadcasted_iota(jnp.int32, (t, t), 0)
        c = lax.broadcasted_iota(jnp.int32, (t, t), 1)
        self.masks = []
        for i in range(self.n_items):
            if i == 0:
                self.masks.append(c < r)
            elif i <= nq // 2:
                self.masks.append(c < r + jnp.where(i == s + 1, 0, t))
            else:
                self.masks.append(None)

    def logits(self):
        t = self.t
        self.zs = [_dot_nt(self.q_ref[pl.ds(self.qrow[i], t), :], self.k_ref[pl.ds(self.krow[i], t), :].astype(BF16))
                   + self.bias for i in range(self.n_items)]

    def weights(self):
        t = self.t
        tri = _tri_strict(t)
        sps = [_sb_softplus(z2, m) for z2, m in zip(self.zs, self.masks)]
        incls = [sp + _dot(sp.astype(BF16), tri) for sp in sps]
        run_a = jnp.zeros((t, 1), F32)
        run_b = jnp.zeros((t, 1), F32)
        self.ws = []
        for i in range(self.n_items):
            rowsum = jnp.sum(sps[i], axis=-1, keepdims=True)
            carry = _pick(self.in_a[i], run_a, run_b)
            self.ws.append(_sb_exp(self.zs[i], incls[i], carry, self.masks[i]))
            run_a = run_a + _pick(self.in_a[i], rowsum, 0.0)
            run_b = run_b + _pick(self.in_a[i], 0.0, rowsum)

    def output(self, za_ref, o_ref):
        t, s = self.t, self.s
        acc_a = jnp.zeros((t, HEAD_DIM), F32)
        acc_b = jnp.zeros((t, HEAD_DIM), F32)
        for i in range(self.n_items):
            pv = _dot(self.ws[i], self.v_ref[pl.ds(self.krow[i], t), :].astype(BF16))
            acc_a = acc_a + _pick(self.in_a[i], pv, 0.0)
            acc_b = acc_b + _pick(self.in_a[i], 0.0, pv)
        for acc, tile in ((acc_a, s), (acc_b, self.nq - 1 - s)):
            rows = pl.ds(pl.multiple_of(tile * t, t), t)
            o_ref[rows, :] = (acc * _silu(za_ref[rows, :])).astype(o_ref.dtype)


def _attn_kernel(pt_ref, bias_ref, q_ref, kn_ref, vn_ref, za_ref, *rest, n_pages, n_heads, t):
    del pt_ref
    k_pages = rest[:n_pages]
    v_pages = rest[n_pages:2 * n_pages]
    qp_ref, kp_ref, vp_ref, zap_ref, o_ref, op_ref = rest[2 * n_pages:]
    nq = qp_ref.shape[0] // t
    n = pl.program_id(0)
    pairs = nq // 2
    head = _imod(_idiv(n, pairs), n_heads)
    sample = _SampleAttn(bias_ref, q_ref, kn_ref, vn_ref, k_pages, v_pages, n_heads)
    prompt = _PromptAttn(bias_ref[0, head], qp_ref, kp_ref, vp_ref, _imod(n, pairs), t)
    sample.logits()
    prompt.logits()
    sample.weights()
    prompt.weights()
    sample.output(za_ref, o_ref)
    prompt.output(zap_ref, op_ref)


def _attn(page_table, bias, q, k_new, v_new, za, cache_k, cache_v, qp, kp, vp, zap, *, n_phys, n_batch, seq):
    n_seq, t_new, width = q.shape
    n_pages = page_table.shape[1]
    n_heads = width // HEAD_DIM
    page = cache_k.shape[0] // (n_phys * n_heads)
    t = SB_BLOCK
    pairs = seq // t // 2
    assert n_seq == n_batch * n_heads * pairs, "one (prompt sequence, head, query-tile pair) per sample sequence"
    seq_spec = pl.BlockSpec((None, t_new, width), lambda n, pt: (n, 0, 0))
    new_spec = pl.BlockSpec((None, t_new * n_heads, HEAD_DIM), lambda n, pt: (n, 0, 0))
    head_spec = pl.BlockSpec((seq, HEAD_DIM), lambda n, pt: (n // (n_heads * pairs), (n // pairs) % n_heads))

    def page_spec(p):
        return pl.BlockSpec((page * n_heads, HEAD_DIM), lambda n, pt: (pt[n, p], 0))

    grid_spec = pltpu.PrefetchScalarGridSpec(
        num_scalar_prefetch=1,
        grid=(n_seq,),
        in_specs=[pl.BlockSpec(memory_space=pltpu.SMEM), seq_spec, new_spec, new_spec, seq_spec]
                 + [page_spec(p) for p in range(n_pages)] * 2 + [head_spec] * 4,
        out_specs=[seq_spec, head_spec],
    )
    return pl.pallas_call(
        functools.partial(_attn_kernel, n_pages=n_pages, n_heads=n_heads, t=t),
        grid_spec=grid_spec,
        out_shape=[jax.ShapeDtypeStruct((n_seq, t_new, width), F32), jax.ShapeDtypeStruct(qp.shape, BF16)],
        compiler_params=_cparams(1),
        name="attn",
    )(page_table, bias, q, k_new, v_new, za, *([cache_k] * n_pages), *([cache_v] * n_pages), qp, kp, vp, zap)


def _residual_ln(x, mix, gate, g, b, alpha):
    r = alpha * x + _gate_mix(mix, gate)
    mu = jnp.mean(r, axis=-1, keepdims=True)
    rc = r - mu
    var = jnp.mean(rc * rc, axis=-1, keepdims=True)
    return rc * lax.rsqrt(var + LN_EPS) * g + b


def _outproj_ln_kernel(*refs, n_in, alpha):
    x_ref, gate_ref = refs[0], refs[1]
    acts = refs[2:2 + n_in]
    ws = refs[2 + n_in:2 + 2 * n_in]
    g_ref, b_ref, y_ref = refs[2 + 2 * n_in:]
    mix = _dot(acts[0][...].astype(BF16), ws[0][...])
    for a, w in zip(acts[1:], ws[1:]):
        mix = mix + _dot(a[...].astype(BF16), w[...])
    y_ref[...] = _residual_ln(x_ref[...], mix, gate_ref[...], g_ref[...], b_ref[...], alpha)


def _outproj_ln(x, gate, acts, w_out, ln_g, ln_b, *, tm, tiles_per_seq, alpha):
    rows, d = x.shape
    gm = gate.shape[-2]
    if gate.ndim == 3:
        mod_spec = pl.BlockSpec((None, gm, d), lambda i: (i // tiles_per_seq, 0, 0))
    else:
        mod_spec = pl.BlockSpec((gm, d), lambda i: (0, 0))
    n_in = len(acts)
    kw = w_out.shape[0] // n_in
    once = pl.Buffered(1)
    w_specs = [pl.BlockSpec((kw, d), functools.partial(lambda i, g: (g, 0), g=g), pipeline_mode=once)
               for g in range(n_in)]
    vec_spec = pl.BlockSpec((1, d), lambda i: (0, 0))
    return pl.pallas_call(
        functools.partial(_outproj_ln_kernel, n_in=n_in, alpha=alpha),
        grid=(rows // tm,),
        in_specs=[pl.BlockSpec((tm, d), lambda i: (i, 0)), mod_spec]
                 + [pl.BlockSpec((tm, kw), lambda i: (i, 0))] * n_in + w_specs + [vec_spec, vec_spec],
        out_specs=pl.BlockSpec((tm, d), lambda i: (i, 0)),
        out_shape=jax.ShapeDtypeStruct((rows, d), F32),
        compiler_params=_cparams(1),
        name="outproj_ln",
    )(x, gate, *acts, *([w_out] * n_in), ln_g.reshape(1, d), ln_b.reshape(1, d))


def _scan_rows(a, b, h_scr, h0):
    rows, d = a.shape
    g = rows // SUBLANES
    a = a.reshape(g, SUBLANES, d)
    b = b.reshape(g, SUBLANES, d)
    row = lax.broadcasted_iota(jnp.int32, (1, SUBLANES, d), 1)
    s = 1
    while s < SUBLANES:
        keep = row >= s
        b = a * jnp.where(keep, pltpu.roll(b, s, 1), 0.0) + b
        a = a * jnp.where(keep, pltpu.roll(a, s, 1), 1.0)
        s *= 2
    h = h0
    for i in range(g):
        hg = a[i] * h + b[i]
        h_scr[i * SUBLANES:(i + 1) * SUBLANES, :] = hg
        h = hg[SUBLANES - 1:]
    return h


def _lru_kernel(x_ref, shift_ref, scale_ref, gate_ref, state_ref, h0_ref, wx_ref, wz_ref, wconv_ref, bconv_ref,
                wga_ref, wgx_ref, bga_ref, bgx_ref, lam_ref, wout_ref, lng_ref, lnb_ref,
                y_ref, tail_ref, hlast_ref, prev_scr, hcar_scr, h_scr, *, stride, tiles_per_seq, alpha):
    i = pl.program_id(0)

    @pl.when(i % tiles_per_seq == 0)
    def _():
        prev_scr[...] = state_ref[...]
        hcar_scr[...] = h0_ref[...]

    u = _modulate(x_ref[...], scale_ref[...], shift_ref[...]).astype(BF16)
    xr = _dot(u, wx_ref[...])
    prev = prev_scr[...]
    wc = wconv_ref[...]
    xc = wc[3:4] * xr + bconv_ref[...]
    for kk in range(1, 4):
        xc = xc + wc[3 - kk:4 - kk] * _shift_rows(xr, prev, kk * stride)
    new_prev = _next_prev(xr, prev)
    prev_scr[...] = new_prev
    tail_ref[...] = new_prev

    xcb = xc.astype(BF16)
    n_blocks, blk, _ = wga_ref.shape
    ga = jnp.concatenate([_dot(xcb[:, n * blk:(n + 1) * blk], wga_ref[n]) for n in range(n_blocks)], axis=1)
    gx = jnp.concatenate([_dot(xcb[:, n * blk:(n + 1) * blk], wgx_ref[n]) for n in range(n_blocks)], axis=1)
    lam = lam_ref[...]
    neg_log_sig = jnp.maximum(-lam, 0.0) + jnp.log1p(jnp.exp(-jnp.abs(lam)))
    log_a = (-LRU_C) * jax.nn.sigmoid(ga + bga_ref[...]) * neg_log_sig
    a = jnp.exp(log_a)
    bt = jnp.sqrt(jnp.tanh(-log_a) * (1.0 + a * a)) * jax.nn.sigmoid(gx + bgx_ref[...]) * xc

    if stride == x_ref.shape[0]:
        h = a * hcar_scr[...] + bt
        hcar_scr[...] = h
    else:
        hcar_scr[...] = _scan_rows(a, bt, h_scr, hcar_scr[...])
        h = h_scr[...]
    hlast_ref[...] = hcar_scr[...]
    z = _dot(u, wz_ref[...])
    mix = _dot((h * _silu(z)).astype(BF16), wout_ref[...])
    y_ref[...] = _residual_ln(x_ref[...], mix, gate_ref[...], lng_ref[...], lnb_ref[...], alpha)


def _lru(x, shift, scale, gate, state, h0, w_in, w_conv, b_conv, w_ga, w_gx, b_ga, b_gx, lam, w_out, ln_g, ln_b, *,
         tm, stride, tiles_per_seq, alpha):
    rows, d = x.shape
    p = state.shape[1]
    hr = h0.shape[1]
    n_i = rows // tm
    gm = shift.shape[-2]
    if shift.ndim == 3:
        mod_spec = pl.BlockSpec((None, gm, d), lambda i: (i // tiles_per_seq, 0, 0))
    else:
        mod_spec = pl.BlockSpec((gm, d), lambda i: (0, 0))
    once = pl.Buffered(1)
    vec_spec = pl.BlockSpec((1, d), lambda i: (0, 0))
    gate_spec = pl.BlockSpec(w_ga.shape, lambda i: (0, 0, 0), pipeline_mode=once)
    scan_scr = (tm, d) if stride != tm else (SUBLANES, 128)
    return pl.pallas_call(
        functools.partial(_lru_kernel, stride=stride, tiles_per_seq=tiles_per_seq, alpha=alpha),
        grid=(n_i,),
        in_specs=[pl.BlockSpec((tm, d), lambda i: (i, 0)), mod_spec, mod_spec, mod_spec,
                  pl.BlockSpec((None, p, d), lambda i: (i // tiles_per_seq, 0, 0)),
                  pl.BlockSpec((None, hr, d), lambda i: (i // tiles_per_seq, 0, 0)),
                  pl.BlockSpec((d, d), lambda i: (0, 0), pipeline_mode=once),
                  pl.BlockSpec((d, d), lambda i: (0, 1), pipeline_mode=once),
                  pl.BlockSpec((w_conv.shape[0], d), lambda i: (0, 0)), vec_spec,
                  gate_spec, gate_spec, vec_spec, vec_spec, vec_spec,
                  pl.BlockSpec((d, d), lambda i: (0, 0), pipeline_mode=once), vec_spec, vec_spec],
        out_specs=[pl.BlockSpec((tm, d), lambda i: (i, 0)),
                   pl.BlockSpec((None, p, d), lambda i: (i, 0, 0)),
                   pl.BlockSpec((None, hr, d), lambda i: (i, 0, 0))],
        out_shape=[jax.ShapeDtypeStruct((rows, d), F32), jax.ShapeDtypeStruct((n_i, p, d), F32),
                   jax.ShapeDtypeStruct((n_i, hr, d), F32)],
        scratch_shapes=[pltpu.VMEM((p, d), F32), pltpu.VMEM((hr, d), F32), pltpu.VMEM(scan_scr, F32)],
        compiler_params=_cparams(1),
        name="lru",
    )(x, shift, scale, gate, state, h0, w_in, w_in, w_conv, b_conv.reshape(1, d), w_ga, w_gx,
      b_ga.reshape(1, d), b_gx.reshape(1, d), lam.reshape(1, d), w_out, ln_g.reshape(1, d), ln_b.reshape(1, d))


def kernel(x_prompt, x_sample, c_prompt, c_sample, cache_k, cache_v, state_conv_b, state_conv_c, state_h, page_table, we_ada, be_ada, we_in, we_sb_bias, we_conv, we_out, ge_ln, be_ln, wo_ada, bo_ada, wo_in, wo_conv, bo_conv, wo_gate_a, bo_gate_a, wo_gate_x, bo_gate_x, wo_lambda, wo_out, go_ln, bo_ln):
    nb, seq, d = x_prompt.shape
    ns, dec, _ = x_sample.shape
    assert we_ada.shape[0] == 1 and wo_ada.shape[0] == 1, "one even and one odd layer"
    n_heads = we_sb_bias.shape[1]
    width = n_heads * HEAD_DIM
    alpha = (2.0 * (we_ada.shape[0] + wo_ada.shape[0])) ** 0.25
    tm = 512
    tm_in0 = 1024

    def to_time_major(a):
        return jnp.swapaxes(a, 0, 1).reshape(a.shape[1] * ns, a.shape[2])

    def to_seq_major(a, t):
        return jnp.swapaxes(a.reshape(t, ns, a.shape[-1]), 0, 1)

    pad = (-(nb + ns)) % SUBLANES
    c_all = jnp.concatenate([c_prompt, c_sample, jnp.zeros((pad, d), F32)], axis=0)
    mods = []
    for w_ada, b_ada in ((we_ada[0], be_ada[0]), (wo_ada[0], bo_ada[0])):
        m = _ada(c_all, w_ada, b_ada)
        parts = [m[:, g * d:(g + 1) * d] for g in range(3)]
        mods.append(([v[:nb].reshape(nb, 1, d) for v in parts], [v[nb:nb + ns] for v in parts]))

    xp = x_prompt.reshape(nb * seq, d)
    xs = to_time_major(x_sample)
    bias = we_sb_bias[0].reshape(1, n_heads)

    (shift_p, scale_p, gate_p), (shift_s, scale_s, gate_s) = mods[0]
    w_in0 = we_in[0].astype(BF16)
    w_out0 = we_out[0].astype(BF16)
    conv_k = we_conv.shape[1]
    qp, kp, vp, zap, obp, tailp = _inproj0(
        xp, shift_p, scale_p, jnp.zeros((nb, SUBLANES, width), F32), w_in0, we_conv[0],
        tm=tm_in0, tn=256, stride=1, tiles_per_seq=seq // tm_in0, q_dtype=BF16)
    conv_b_prompt = tailp.reshape(nb, seq // tm_in0, SUBLANES, width)[:, -1, SUBLANES - (conv_k - 1):][None]

    state_b = to_time_major(state_conv_b[0])[None]
    rows_s = ns * dec
    tiles_s = rows_s // tm
    qs, ks, vs, zas, obs, tails = _inproj0(
        xs, shift_s, scale_s, state_b, w_in0, we_conv[0],
        tm=tm, tn=256, stride=ns, tiles_per_seq=tiles_s, q_dtype=F32)
    k_sample = to_seq_major(ks, dec).reshape(1, ns, dec, n_heads, HEAD_DIM)
    v_sample = to_seq_major(vs, dec).reshape(1, ns, dec, n_heads, HEAD_DIM)
    n_phys = cache_k.shape[1]
    oas, oap = _attn(page_table, bias, to_seq_major(qs, dec), k_sample.reshape(ns, dec * n_heads, HEAD_DIM),
                     v_sample.reshape(ns, dec * n_heads, HEAD_DIM), to_seq_major(zas, dec),
                     cache_k.reshape(-1, HEAD_DIM), cache_v.reshape(-1, HEAD_DIM), qp, kp, vp, zap,
                     n_phys=n_phys, n_batch=nb, seq=seq)
    yp = _outproj_ln(xp, gate_p, [oap, obp], w_out0, ge_ln[0], be_ln[0], tm=tm, tiles_per_seq=seq // tm,
                     alpha=alpha)
    ys = _outproj_ln(xs, gate_s, [to_time_major(oas), obs], w_out0, ge_ln[0], be_ln[0], tm=tm,
                     tiles_per_seq=tiles_s, alpha=alpha)
    conv_b_sample = to_seq_major(tails[-1], conv_k - 1)[None]

    (shift_p, scale_p, gate_p), (shift_s, scale_s, gate_s) = mods[1]
    w_in1 = wo_in[0].astype(BF16)
    w_out1 = wo_out[0].astype(BF16)
    w_ga = wo_gate_a[0].astype(BF16)
    w_gx = wo_gate_x[0].astype(BF16)
    conv_k = wo_conv.shape[1]
    lru_w = (w_in1, wo_conv[0], bo_conv[0], w_ga, w_gx, bo_gate_a[0], bo_gate_x[0], wo_lambda[0],
             w_out1, go_ln[0], bo_ln[0])
    tm1 = 256
    yp, tailp, hp = _lru(yp, shift_p, scale_p, gate_p, jnp.zeros((nb, SUBLANES, d), F32),
                         jnp.zeros((nb, 1, d), F32), *lru_w, tm=tm1, stride=1, tiles_per_seq=seq // tm1, alpha=alpha)
    conv_c_prompt = tailp.reshape(nb, seq // tm1, SUBLANES, d)[:, -1, SUBLANES - (conv_k - 1):][None]
    h_prompt = hp.reshape(nb, seq // tm1, d)[:, -1][None]

    state_c = to_time_major(state_conv_c[0])[None]
    ys, tails, hs = _lru(ys, shift_s, scale_s, gate_s, state_c, state_h[0][None], *lru_w, tm=ns, stride=ns,
                         tiles_per_seq=dec, alpha=alpha)
    conv_c_sample = to_seq_major(tails[-1], conv_k - 1)[None]
    h_sample = hs[-1][None]

    y_prompt = yp.reshape(nb, seq, d)
    y_sample = to_seq_major(ys, dec)
    k_prompt = kp.reshape(1, nb, seq, n_heads, HEAD_DIM)
    v_prompt = vp.reshape(1, nb, seq, n_heads, HEAD_DIM)
    return (y_prompt, y_sample, k_prompt, v_prompt, conv_b_prompt, conv_c_prompt, h_prompt,
            k_sample, v_sample, conv_b_sample, conv_c_sample, h_sample)
```

```python
import functools
import math

import jax
import jax.numpy as jnp
from jax import lax
from jax.experimental import pallas as pl
from jax.experimental.pallas import tpu as pltpu

F32 = jnp.float32
BF16 = jnp.bfloat16

HEAD_DIM = 128
SB_BLOCK = 256
LRU_C = 8.0
LN_EPS = 1e-5
LOG2E = math.log2(math.e)
Q_SCALE = LOG2E / math.sqrt(HEAD_DIM)
SUBLANES = 8
VMEM_LIMIT = 58 * 1024 * 1024


def _cparams(n_axes):
    return pltpu.CompilerParams(dimension_semantics=("arbitrary",) * n_axes, vmem_limit_bytes=VMEM_LIMIT)


def _silu(x):
    return x * jax.nn.sigmoid(x)


def _idiv(x, n):
    assert n & (n - 1) == 0
    return x >> (n.bit_length() - 1)


def _imod(x, n):
    assert n & (n - 1) == 0
    return x & (n - 1)


def _dot(a, b):
    return jnp.dot(a, b, preferred_element_type=F32)


def _dot_nt(a, b):
    return lax.dot_general(a, b, (((1,), (1,)), ((), ())), preferred_element_type=F32)


def _modulate(x, scale, shift):
    gm = scale.shape[0]
    if gm == 1:
        return x * (1.0 + scale) + shift
    rows, d = x.shape
    x3 = x.reshape(rows // gm, gm, d)
    return (x3 * (1.0 + scale)[None] + shift[None]).reshape(rows, d)


def _gate_mix(mix, gate):
    gm = gate.shape[0]
    if gm == 1:
        return mix * (1.0 + gate)
    rows, d = mix.shape
    return (mix.reshape(rows // gm, gm, d) * (1.0 + gate)[None]).reshape(rows, d)


def _shift_rows(cur, prev, d):
    tm, p = cur.shape[0], prev.shape[0]
    if d % SUBLANES == 0:
        if d >= tm:
            return prev[p - d:p - d + tm]
        return jnp.concatenate([prev[p - d:], cur[:tm - d]], axis=0)
    assert d < SUBLANES <= p
    rolled = pltpu.roll(cur, d, 0)
    prev_rolled = pltpu.roll(prev[p - SUBLANES:], d, 0)
    row = lax.broadcasted_iota(jnp.int32, prev_rolled.shape, 0)
    head = jnp.where(row < d, prev_rolled, rolled[:SUBLANES])
    return jnp.concatenate([head, rolled[SUBLANES:]], axis=0)


def _next_prev(cur, prev):
    tm, p = cur.shape[0], prev.shape[0]
    if tm >= p:
        return cur[tm - p:]
    return jnp.concatenate([prev[tm:], cur], axis=0)


def _ada_kernel(c_ref, w0_ref, w1_ref, b0_ref, b1_ref, o_ref):
    layer = pl.program_id(0)
    k = pl.program_id(1)
    s = _silu(c_ref[...]).astype(BF16)
    for this, w_ref, b_ref in ((0, w0_ref, b0_ref), (1, w1_ref, b1_ref)):
        @pl.when((layer == this) & (k == 0))
        def _():
            o_ref[...] = _dot(s, w_ref[...].astype(BF16)) + b_ref[...]

        @pl.when((layer == this) & (k > 0))
        def _():
            o_ref[...] += _dot(s, w_ref[...].astype(BF16))


def _ada(c_all, w0, b0, w1, b1, tk=256):
    rows, d = c_all.shape
    n = w0.shape[1]
    nk = d // tk
    return pl.pallas_call(
        _ada_kernel,
        grid=(2, nk),
        in_specs=[pl.BlockSpec((rows, tk), lambda l, k: (0, k)),
                  pl.BlockSpec((tk, n), lambda l, k: (k * (1 - l) + (nk - 1) * l, 0)),
                  pl.BlockSpec((tk, n), lambda l, k: (k * l, 0)),
                  pl.BlockSpec((1, n), lambda l, k: (0, 0)),
                  pl.BlockSpec((1, n), lambda l, k: (0, 0))],
        out_specs=pl.BlockSpec((None, rows, n), lambda l, k: (l, 0, 0)),
        out_shape=jax.ShapeDtypeStruct((2, rows, n), F32),
        compiler_params=_cparams(2),
        name="ada",
    )(c_all, w0, w1, b0.reshape(1, n), b1.reshape(1, n))


def _inproj0_kernel(x_ref, shift_ref, scale_ref, state_ref, wq, wk, wv, wza, wbg, wcg, wxi, wzb, wconv_ref,
                    q_ref, k_ref, v_ref, za_ref, ob_ref, tail_ref, u_scr, carry_scr, *, stride, tiles_per_seq):
    i = pl.program_id(0)
    j = pl.program_id(1)

    @pl.when(j == 0)
    def _():
        u_scr[...] = _modulate(x_ref[...], scale_ref[...], shift_ref[...]).astype(BF16)

    @pl.when(i % tiles_per_seq == 0)
    def _():
        carry_scr[j] = state_ref[...]

    u = u_scr[...]
    q_ref[...] = (_dot(u, wq[...]) * Q_SCALE).astype(q_ref.dtype)
    k_ref[...] = _dot(u, wk[...])
    v_ref[...] = _dot(u, wv[...])
    za_ref[...] = _dot(u, wza[...])
    ci = _dot(u, wcg[...]) * _dot(u, wxi[...])
    prev = carry_scr[j]
    wc = wconv_ref[...]
    conv = wc[2:3] * ci + wc[1:2] * _shift_rows(ci, prev, stride) + wc[0:1] * _shift_rows(ci, prev, 2 * stride)
    ob_ref[...] = (_dot(u, wbg[...]) * conv * _silu(_dot(u, wzb[...]))).astype(ob_ref.dtype)
    new_prev = _next_prev(ci, prev)
    carry_scr[j] = new_prev
    tail_ref[...] = new_prev


def _inproj0(x, shift, scale, state, w_in, w_conv, *, tm, tn, stride, tiles_per_seq, q_dtype):
    rows, d = x.shape
    c = w_conv.shape[1]
    p = state.shape[1]
    n_i, n_j = rows // tm, c // tn
    gm = shift.shape[-2]
    if shift.ndim == 3:
        mod_spec = pl.BlockSpec((None, gm, d), lambda i, j: (i // tiles_per_seq, 0, 0))
    else:
        mod_spec = pl.BlockSpec((gm, d), lambda i, j: (0, 0))

    def w_spec(g):
        return pl.BlockSpec((d, tn), lambda i, j: (0, g * n_j + j))

    act_spec = pl.BlockSpec((tm, tn), lambda i, j: (i, j))
    outs = pl.pallas_call(
        functools.partial(_inproj0_kernel, stride=stride, tiles_per_seq=tiles_per_seq),
        grid=(n_i, n_j),
        in_specs=[pl.BlockSpec((tm, d), lambda i, j: (i, 0)), mod_spec, mod_spec,
                  pl.BlockSpec((None, p, tn), lambda i, j: (i // tiles_per_seq, 0, j))]
                 + [w_spec(g) for g in range(8)]
                 + [pl.BlockSpec((w_conv.shape[0], tn), lambda i, j: (0, j))],
        out_specs=[act_spec] * 5 + [pl.BlockSpec((None, p, tn), lambda i, j: (i, 0, j))],
        out_shape=[jax.ShapeDtypeStruct((rows, c), q_dtype)] + [jax.ShapeDtypeStruct((rows, c), F32)] * 3
                  + [jax.ShapeDtypeStruct((rows, c), BF16), jax.ShapeDtypeStruct((n_i, p, c), F32)],
        scratch_shapes=[pltpu.VMEM((tm, d), BF16), pltpu.VMEM((n_j, p, tn), F32)],
        compiler_params=_cparams(2),
        name="inproj0",
    )(x, shift, scale, state, *([w_in] * 8), w_conv)
    return outs


def _sb_softplus(z2, mask):
    neg_abs = lax.bitcast_convert_type(lax.bitcast_convert_type(z2, jnp.uint32) | jnp.uint32(0x80000000), F32)
    sp = jnp.maximum(z2, 0.0) + jnp.log2(1.0 + jnp.exp2(neg_abs))
    return sp if mask is None else jnp.where(mask, sp, 0.0)


def _sb_exp(z2, incl, carry, mask):
    w = jnp.exp2(z2 - incl - carry)
    return (w if mask is None else jnp.where(mask, w, 0.0)).astype(BF16)


def _tri_strict(t):
    r = lax.broadcasted_iota(jnp.int32, (t, t), 0)
    c = lax.broadcasted_iota(jnp.int32, (t, t), 1)
    return jnp.where(r > c, 1.0, 0.0).astype(BF16)


def _suffix_sum_heads(x, n_heads):
    n = x.shape[1]
    lane = lax.broadcasted_iota(jnp.int32, x.shape, 1)
    s = n_heads
    while s < n:
        x = x + jnp.where(lane < n - s, pltpu.roll(x, n - s, 1), 0.0)
        s *= 2
    return x


def _bcast_heads(v, n_heads):
    lane = lax.broadcasted_iota(jnp.int32, v.shape, 1)
    v = jnp.where(lane < n_heads, v, 0.0)
    s = n_heads
    while s < v.shape[1]:
        v = v + pltpu.roll(v, s, 1)
        s *= 2
    return v


class _SampleAttn:
    def __init__(self, bias_ref, q_ref, kn_ref, vn_ref, k_pages, v_pages, n_heads):
        self.n_heads = n_heads
        self.t_new = t_new = q_ref.shape[0]
        self.k_pages, self.v_pages, self.kn_ref, self.vn_ref = k_pages, v_pages, kn_ref, vn_ref
        n_pages = len(k_pages)
        rows_pp = k_pages[0].shape[0]
        self.blocks = [n_pages] + list(range(n_pages - 1, -1, -1))
        self.sizes = [HEAD_DIM if j == n_pages else rows_pp for j in self.blocks]
        q = q_ref[...]
        self.q_rows = jnp.concatenate([q[:, h * HEAD_DIM:(h + 1) * HEAD_DIM] for h in range(n_heads)],
                                      axis=0).astype(BF16)
        row = lax.broadcasted_iota(jnp.int32, (t_new, HEAD_DIM), 0)
        lanes = {n: lax.broadcasted_iota(jnp.int32, (t_new, n), 1) for n in set(self.sizes)}
        self.heads_of = {n: _imod(lanes[n], n_heads) for n in set(self.sizes)}
        self.biases = {}
        for n in set(self.sizes):
            bias = jnp.full((1, n), bias_ref[0, 0] * LOG2E, F32)
            for h in range(1, n_heads):
                bias = jnp.where(self.heads_of[n][0:1] == h, bias_ref[0, h] * LOG2E, bias)
            self.biases[n] = bias
        new_mask = _idiv(lanes[HEAD_DIM], n_heads) < row
        self.masks = [new_mask if j == n_pages else None for j in self.blocks]

    def _block(self, pages, new_ref, j):
        if j < len(pages):
            return pages[j][...].astype(BF16)
        pad = jnp.zeros((HEAD_DIM - new_ref.shape[0], HEAD_DIM), F32)
        return jnp.concatenate([new_ref[...], pad], axis=0).astype(BF16)

    def logits(self):
        t_new = self.t_new
        self.z2s = []
        for j, n in zip(self.blocks, self.sizes):
            s = _dot_nt(self.q_rows, self._block(self.k_pages, self.kn_ref, j))
            z2 = s[0:t_new]
            for h in range(1, self.n_heads):
                z2 = jnp.where(self.heads_of[n] == h, s[h * t_new:(h + 1) * t_new], z2)
            self.z2s.append(z2 + self.biases[n])

    def weights(self):
        sps = [_sb_softplus(z2, m) for z2, m in zip(self.z2s, self.masks)]
        incls = [_suffix_sum_heads(sp, self.n_heads) for sp in sps]
        totals = [_bcast_heads(incl[:, 0:HEAD_DIM], self.n_heads) for incl in incls]
        carry = jnp.zeros((self.t_new, HEAD_DIM), F32)
        self.w_rows = []
        for i, n in enumerate(self.sizes):
            w = jnp.exp2(self.z2s[i] - incls[i] - jnp.concatenate([carry] * (n // HEAD_DIM), axis=1))
            if self.masks[i] is not None:
                w = jnp.where(self.masks[i], w, 0.0)
            self.w_rows.append(jnp.concatenate([jnp.where(self.heads_of[n] == h, w, 0.0)
                                                for h in range(self.n_heads)], axis=0).astype(BF16))
            carry = carry + totals[i]

    def output(self, za_ref, o_ref):
        t_new = self.t_new
        acc = jnp.zeros((self.n_heads * t_new, HEAD_DIM), F32)
        for j, wr in zip(self.blocks, self.w_rows):
            acc = acc + _dot(wr, self._block(self.v_pages, self.vn_ref, j))
        o = jnp.concatenate([acc[h * t_new:(h + 1) * t_new] for h in range(self.n_heads)], axis=1)
        o_ref[...] = (o * _silu(za_ref[...])).astype(o_ref.dtype)


def _pick(cond, a, b):
    if isinstance(cond, bool):
        return a if cond else b
    return jnp.where(cond, a, b)


class _PromptAttn:
    def __init__(self, bias, q_ref, k_ref, v_ref, s, t):
        self.q_ref, self.k_ref, self.v_ref, self.s, self.t = q_ref, k_ref, v_ref, s, t
        self.nq = nq = q_ref.shape[0] // t
        self.n_items = nq + 1
        self.bias = bias * LOG2E
        self.in_a = [True if i == 0 else False if i >= nq // 2 else i <= s for i in range(self.n_items)]
        self.qrow = [pl.multiple_of(_pick(self.in_a[i], s, nq - 1 - s) * t, t) for i in range(self.n_items)]
        self.krow = [pl.multiple_of(_pick(self.in_a[i], s - i, nq - i) * t, t) for i in range(self.n_items)]
        r = lax.broadcasted_iota(jnp.int32, (t, t), 0)
        c = lax.broadcasted_iota(jnp.int32, (t, t), 1)
        self.masks = []
        for i in range(self.n_items):
            if i == 0:
                self.masks.append(c < r)
            elif i <= nq // 2:
                self.masks.append(c < r + jnp.where(i == s + 1, 0, t))
            else:
                self.masks.append(None)

    def logits(self):
        t = self.t
        self.zs = [_dot_nt(self.q_ref[pl.ds(self.qrow[i], t), :], self.k_ref[pl.ds(self.krow[i], t), :].astype(BF16))
                   + self.bias for i in range(self.n_items)]

    def weights(self):
        t = self.t
        tri = _tri_strict(t)
        sps = [_sb_softplus(z2, m) for z2, m in zip(self.zs, self.masks)]
        incls = [sp + _dot(sp.astype(BF16), tri) for sp in sps]
        run_a = jnp.zeros((t, 1), F32)
        run_b = jnp.zeros((t, 1), F32)
        self.ws = []
        for i in range(self.n_items):
            rowsum = jnp.sum(sps[i], axis=-1, keepdims=True)
            carry = _pick(self.in_a[i], run_a, run_b)
            self.ws.append(_sb_exp(self.zs[i], incls[i], carry, self.masks[i]))
            run_a = run_a + _pick(self.in_a[i], rowsum, 0.0)
            run_b = run_b + _pick(self.in_a[i], 0.0, rowsum)

    def output(self, za_ref, o_ref):
        t, s = self.t, self.s
        acc_a = jnp.zeros((t, HEAD_DIM), F32)
        acc_b = jnp.zeros((t, HEAD_DIM), F32)
        for i in range(self.n_items):
            pv = _dot(self.ws[i], self.v_ref[pl.ds(self.krow[i], t), :].astype(BF16))
            acc_a = acc_a + _pick(self.in_a[i], pv, 0.0)
            acc_b = acc_b + _pick(self.in_a[i], 0.0, pv)
        for acc, tile in ((acc_a, s), (acc_b, self.nq - 1 - s)):
            rows = pl.ds(pl.multiple_of(tile * t, t), t)
            o_ref[rows, :] = (acc * _silu(za_ref[rows, :])).astype(o_ref.dtype)


def _attn_kernel(pt_ref, bias_ref, q_ref, kn_ref, vn_ref, za_ref, *rest, n_pages, n_heads, t):
    del pt_ref
    k_pages = rest[:n_pages]
    v_pages = rest[n_pages:2 * n_pages]
    qp_ref, kp_ref, vp_ref, zap_ref, o_ref, op_ref = rest[2 * n_pages:]
    nq = qp_ref.shape[0] // t
    n = pl.program_id(0)
    pairs = nq // 2
    head = _imod(_idiv(n, pairs), n_heads)
    sample = _SampleAttn(bias_ref, q_ref, kn_ref, vn_ref, k_pages, v_pages, n_heads)
    prompt = _PromptAttn(bias_ref[0, head], qp_ref, kp_ref, vp_ref, _imod(n, pairs), t)
    sample.logits()
    prompt.logits()
    sample.weights()
    prompt.weights()
    sample.output(za_ref, o_ref)
    prompt.output(zap_ref, op_ref)


def _attn(page_table, bias, q, k_new, v_new, za, cache_k, cache_v, qp, kp, vp, zap, *, n_phys, n_batch, seq):
    n_seq, t_new, width = q.shape
    n_pages = page_table.shape[1]
    n_heads = width // HEAD_DIM
    page = cache_k.shape[0] // (n_phys * n_heads)
    t = SB_BLOCK
    pairs = seq // t // 2
    assert n_seq == n_batch * n_heads * pairs, "one (prompt sequence, head, query-tile pair) per sample sequence"
    seq_spec = pl.BlockSpec((None, t_new, width), lambda n, pt: (n, 0, 0))
    new_spec = pl.BlockSpec((None, t_new * n_heads, HEAD_DIM), lambda n, pt: (n, 0, 0))
    head_spec = pl.BlockSpec((seq, HEAD_DIM), lambda n, pt: (n // (n_heads * pairs), (n // pairs) % n_heads))

    def page_spec(p):
        return pl.BlockSpec((page * n_heads, HEAD_DIM), lambda n, pt: (pt[n, p], 0))

    grid_spec = pltpu.PrefetchScalarGridSpec(
        num_scalar_prefetch=1,
        grid=(n_seq,),
        in_specs=[pl.BlockSpec(memory_space=pltpu.SMEM), seq_spec, new_spec, new_spec, seq_spec]
                 + [page_spec(p) for p in range(n_pages)] * 2 + [head_spec] * 4,
        out_specs=[seq_spec, head_spec],
    )
    return pl.pallas_call(
        functools.partial(_attn_kernel, n_pages=n_pages, n_heads=n_heads, t=t),
        grid_spec=grid_spec,
        out_shape=[jax.ShapeDtypeStruct((n_seq, t_new, width), F32), jax.ShapeDtypeStruct(qp.shape, BF16)],
        compiler_params=_cparams(1),
        name="attn",
    )(page_table, bias, q, k_new, v_new, za, *([cache_k] * n_pages), *([cache_v] * n_pages), qp, kp, vp, zap)


def _residual_ln(x, mix, gate, g, b, alpha):
    r = alpha * x + _gate_mix(mix, gate)
    mu = jnp.mean(r, axis=-1, keepdims=True)
    rc = r - mu
    var = jnp.mean(rc * rc, axis=-1, keepdims=True)
    return rc * lax.rsqrt(var + LN_EPS) * g + b


def _outproj_ln_kernel(*refs, n_in, n_cast, alpha):
    x_ref, gate_ref = refs[0], refs[1]
    acts = refs[2:2 + n_in]
    ws = refs[2 + n_in:2 + 2 * n_in]
    g_ref, b_ref = refs[2 + 2 * n_in:4 + 2 * n_in]
    cast_src = refs[4 + 2 * n_in:4 + 2 * n_in + n_cast]
    y_ref = refs[4 + 2 * n_in + n_cast]
    cast_dst = refs[5 + 2 * n_in + n_cast:]
    mix = _dot(acts[0][...].astype(BF16), ws[0][...])
    for a, w in zip(acts[1:], ws[1:]):
        mix = mix + _dot(a[...].astype(BF16), w[...])
    y_ref[...] = _residual_ln(x_ref[...], mix, gate_ref[...], g_ref[...], b_ref[...], alpha)
    for src, dst in zip(cast_src, cast_dst):
        dst[...] = src[...].astype(dst.dtype)


def _outproj_ln(x, gate, acts, w_out, ln_g, ln_b, *, tm, tiles_per_seq, alpha, casts=()):
    rows, d = x.shape
    n_steps = rows // tm
    gm = gate.shape[-2]
    if gate.ndim == 3:
        mod_spec = pl.BlockSpec((None, gm, d), lambda i: (i // tiles_per_seq, 0, 0))
    else:
        mod_spec = pl.BlockSpec((gm, d), lambda i: (0, 0))
    n_in = len(acts)
    kw = w_out.shape[0] // n_in
    once = pl.Buffered(1)
    w_specs = [pl.BlockSpec((kw, d), functools.partial(lambda i, g: (g, 0), g=g), pipeline_mode=once)
               for g in range(n_in)]
    vec_spec = pl.BlockSpec((1, d), lambda i: (0, 0))
    cast_specs = [pl.BlockSpec((c.shape[0] // n_steps, c.shape[1]), lambda i: (i, 0)) for c in casts]
    outs = pl.pallas_call(
        functools.partial(_outproj_ln_kernel, n_in=n_in, n_cast=len(casts), alpha=alpha),
        grid=(n_steps,),
        in_specs=[pl.BlockSpec((tm, d), lambda i: (i, 0)), mod_spec]
                 + [pl.BlockSpec((tm, kw), lambda i: (i, 0))] * n_in + w_specs + [vec_spec, vec_spec] + cast_specs,
        out_specs=[pl.BlockSpec((tm, d), lambda i: (i, 0))] + cast_specs,
        out_shape=[jax.ShapeDtypeStruct((rows, d), F32)] + [jax.ShapeDtypeStruct(c.shape, BF16) for c in casts],
        compiler_params=_cparams(1),
        name="outproj_ln",
    )(x, gate, *acts, *([w_out] * n_in), ln_g.reshape(1, d), ln_b.reshape(1, d), *casts)
    return outs[0] if not casts else tuple(outs)


def _scan_rows(a, b, h_scr, h0):
    rows, d = a.shape
    g = rows // SUBLANES
    a = a.reshape(g, SUBLANES, d)
    b = b.reshape(g, SUBLANES, d)
    row = lax.broadcasted_iota(jnp.int32, (1, SUBLANES, d), 1)
    s = 1
    while s < SUBLANES:
        keep = row >= s
        b = a * jnp.where(keep, pltpu.roll(b, s, 1), 0.0) + b
        a = a * jnp.where(keep, pltpu.roll(a, s, 1), 1.0)
        s *= 2
    h = h0
    for i in range(g):
        hg = a[i] * h + b[i]
        h_scr[i * SUBLANES:(i + 1) * SUBLANES, :] = hg
        h = hg[SUBLANES - 1:]
    return h


def _lru_kernel(x_ref, shift_ref, scale_ref, gate_ref, state_ref, h0_ref, wx_ref, wz_ref, wconv_ref, bconv_ref,
                wga_ref, wgx_ref, bga_ref, bgx_ref, lam_ref, wout_ref, lng_ref, lnb_ref,
                y_ref, tail_ref, hlast_ref, prev_scr, hcar_scr, h_scr, *, stride, tiles_per_seq, alpha):
    i = pl.program_id(0)

    @pl.when(i % tiles_per_seq == 0)
    def _():
        prev_scr[...] = state_ref[...]
        hcar_scr[...] = h0_ref[...]

    u = _modulate(x_ref[...], scale_ref[...], shift_ref[...]).astype(BF16)
    xr = _dot(u, wx_ref[...])
    prev = prev_scr[...]
    wc = wconv_ref[...]
    xc = wc[3:4] * xr + bconv_ref[...]
    for kk in range(1, 4):
        xc = xc + wc[3 - kk:4 - kk] * _shift_rows(xr, prev, kk * stride)
    new_prev = _next_prev(xr, prev)
    prev_scr[...] = new_prev
    tail_ref[...] = new_prev

    xcb = xc.astype(BF16)
    n_blocks, blk, _ = wga_ref.shape
    ga = jnp.concatenate([_dot(xcb[:, n * blk:(n + 1) * blk], wga_ref[n]) for n in range(n_blocks)], axis=1)
    gx = jnp.concatenate([_dot(xcb[:, n * blk:(n + 1) * blk], wgx_ref[n]) for n in range(n_blocks)], axis=1)
    lam = lam_ref[...]
    neg_log_sig = jnp.maximum(-lam, 0.0) + jnp.log1p(jnp.exp(-jnp.abs(lam)))
    log_a = (-LRU_C) * jax.nn.sigmoid(ga + bga_ref[...]) * neg_log_sig
    a = jnp.exp(log_a)
    bt = jnp.sqrt(jnp.tanh(-log_a) * (1.0 + a * a)) * jax.nn.sigmoid(gx + bgx_ref[...]) * xc

    if stride == x_ref.shape[0]:
        h = a * hcar_scr[...] + bt
        hcar_scr[...] = h
    else:
        hcar_scr[...] = _scan_rows(a, bt, h_scr, hcar_scr[...])
        h = h_scr[...]
    hlast_ref[...] = hcar_scr[...]
    z = _dot(u, wz_ref[...])
    mix = _dot((h * _silu(z)).astype(BF16), wout_ref[...])
    y_ref[...] = _residual_ln(x_ref[...], mix, gate_ref[...], lng_ref[...], lnb_ref[...], alpha)


def _lru(x, shift, scale, gate, state, h0, w_in, w_conv, b_conv, w_ga, w_gx, b_ga, b_gx, lam, w_out, ln_g, ln_b, *,
         tm, stride, tiles_per_seq, alpha):
    rows, d = x.shape
    p = state.shape[1]
    hr = h0.shape[1]
    n_i = rows // tm
    gm = shift.shape[-2]
    if shift.ndim == 3:
        mod_spec = pl.BlockSpec((None, gm, d), lambda i: (i // tiles_per_seq, 0, 0))
    else:
        mod_spec = pl.BlockSpec((gm, d), lambda i: (0, 0))
    once = pl.Buffered(1)
    vec_spec = pl.BlockSpec((1, d), lambda i: (0, 0))
    gate_spec = pl.BlockSpec(w_ga.shape, lambda i: (0, 0, 0), pipeline_mode=once)
    scan_scr = (tm, d) if stride != tm else (SUBLANES, 128)
    return pl.pallas_call(
        functools.partial(_lru_kernel, stride=stride, tiles_per_seq=tiles_per_seq, alpha=alpha),
        grid=(n_i,),
        in_specs=[pl.BlockSpec((tm, d), lambda i: (i, 0)), mod_spec, mod_spec, mod_spec,
                  pl.BlockSpec((None, p, d), lambda i: (i // tiles_per_seq, 0, 0)),
                  pl.BlockSpec((None, hr, d), lambda i: (i // tiles_per_seq, 0, 0)),
                  pl.BlockSpec((d, d), lambda i: (0, 0), pipeline_mode=once),
                  pl.BlockSpec((d, d), lambda i: (0, 1), pipeline_mode=once),
                  pl.BlockSpec((w_conv.shape[0], d), lambda i: (0, 0)), vec_spec,
                  gate_spec, gate_spec, vec_spec, vec_spec, vec_spec,
                  pl.BlockSpec((d, d), lambda i: (0, 0), pipeline_mode=once), vec_spec, vec_spec],
        out_specs=[pl.BlockSpec((tm, d), lambda i: (i, 0)),
                   pl.BlockSpec((None, p, d), lambda i: (i, 0, 0)),
                   pl.BlockSpec((None, hr, d), lambda i: (i, 0, 0))],
        out_shape=[jax.ShapeDtypeStruct((rows, d), F32), jax.ShapeDtypeStruct((n_i, p, d), F32),
                   jax.ShapeDtypeStruct((n_i, hr, d), F32)],
        scratch_shapes=[pltpu.VMEM((p, d), F32), pltpu.VMEM((hr, d), F32), pltpu.VMEM(scan_scr, F32)],
        compiler_params=_cparams(1),
        name="lru",
    )(x, shift, scale, gate, state, h0, w_in, w_in, w_conv, b_conv.reshape(1, d), w_ga, w_gx,
      b_ga.reshape(1, d), b_gx.reshape(1, d), lam.reshape(1, d), w_out, ln_g.reshape(1, d), ln_b.reshape(1, d))


def kernel(x_prompt, x_sample, c_prompt, c_sample, cache_k, cache_v, state_conv_b, state_conv_c, state_h, page_table, we_ada, be_ada, we_in, we_sb_bias, we_conv, we_out, ge_ln, be_ln, wo_ada, bo_ada, wo_in, wo_conv, bo_conv, wo_gate_a, bo_gate_a, wo_gate_x, bo_gate_x, wo_lambda, wo_out, go_ln, bo_ln):
    nb, seq, d = x_prompt.shape
    ns, dec, _ = x_sample.shape
    assert we_ada.shape[0] == 1 and wo_ada.shape[0] == 1, "one even and one odd layer"
    n_heads = we_sb_bias.shape[1]
    width = n_heads * HEAD_DIM
    alpha = (2.0 * (we_ada.shape[0] + wo_ada.shape[0])) ** 0.25
    tm = 512
    tm_in0 = 1024

    def to_time_major(a):
        return jnp.swapaxes(a, 0, 1).reshape(a.shape[1] * ns, a.shape[2])

    def to_seq_major(a, t):
        return jnp.swapaxes(a.reshape(t, ns, a.shape[-1]), 0, 1)

    pad = (-(nb + ns)) % SUBLANES
    c_all = jnp.concatenate([c_prompt, c_sample, jnp.zeros((pad, d), F32)], axis=0)
    mods = []
    m_all = _ada(c_all, we_ada[0], be_ada[0], wo_ada[0], bo_ada[0])
    for m in (m_all[0], m_all[1]):
        parts = [m[:, g * d:(g + 1) * d] for g in range(3)]
        mods.append(([v[:nb].reshape(nb, 1, d) for v in parts], [v[nb:nb + ns] for v in parts]))

    xp = x_prompt.reshape(nb * seq, d)
    xs = to_time_major(x_sample)
    bias = we_sb_bias[0].reshape(1, n_heads)

    (shift_p, scale_p, gate_p), (shift_s, scale_s, gate_s) = mods[0]
    w_in0 = we_in[0].astype(BF16)
    w_out0 = we_out[0].astype(BF16)
    conv_k = we_conv.shape[1]
    qp, kp, vp, zap, obp, tailp = _inproj0(
        xp, shift_p, scale_p, jnp.zeros((nb, SUBLANES, width), F32), w_in0, we_conv[0],
        tm=tm_in0, tn=256, stride=1, tiles_per_seq=seq // tm_in0, q_dtype=BF16)
    conv_b_prompt = tailp.reshape(nb, seq // tm_in0, SUBLANES, width)[:, -1, SUBLANES - (conv_k - 1):][None]

    state_b = to_time_major(state_conv_b[0])[None]
    rows_s = ns * dec
    tiles_s = rows_s // tm
    qs, ks, vs, zas, obs, tails = _inproj0(
        xs, shift_s, scale_s, state_b, w_in0, we_conv[0],
        tm=tm, tn=256, stride=ns, tiles_per_seq=tiles_s, q_dtype=F32)
    k_sample = to_seq_major(ks, dec).reshape(1, ns, dec, n_heads, HEAD_DIM)
    v_sample = to_seq_major(vs, dec).reshape(1, ns, dec, n_heads, HEAD_DIM)
    n_phys = cache_k.shape[1]
    oas, oap = _attn(page_table, bias, to_seq_major(qs, dec), k_sample.reshape(ns, dec * n_heads, HEAD_DIM),
                     v_sample.reshape(ns, dec * n_heads, HEAD_DIM), to_seq_major(zas, dec),
                     cache_k.reshape(-1, HEAD_DIM), cache_v.reshape(-1, HEAD_DIM), qp, kp, vp, zap,
                     n_phys=n_phys, n_batch=nb, seq=seq)
    yp, w_in1, w_out1 = _outproj_ln(xp, gate_p, [oap, obp], w_out0, ge_ln[0], be_ln[0], tm=tm,
                                    tiles_per_seq=seq // tm, alpha=alpha, casts=(wo_in[0], wo_out[0]))
    ys = _outproj_ln(xs, gate_s, [to_time_major(oas), obs], w_out0, ge_ln[0], be_ln[0], tm=tm,
                     tiles_per_seq=tiles_s, alpha=alpha)
    conv_b_sample = to_seq_major(tails[-1], conv_k - 1)[None]

    (shift_p, scale_p, gate_p), (shift_s, scale_s, gate_s) = mods[1]
    w_ga = wo_gate_a[0].astype(BF16)
    w_gx = wo_gate_x[0].astype(BF16)
    conv_k = wo_conv.shape[1]
    lru_w = (w_in1, wo_conv[0], bo_conv[0], w_ga, w_gx, bo_gate_a[0], bo_gate_x[0], wo_lambda[0],
             w_out1, go_ln[0], bo_ln[0])
    tm1 = 256
    yp, tailp, hp = _lru(yp, shift_p, scale_p, gate_p, jnp.zeros((nb, SUBLANES, d), F32),
                         jnp.zeros((nb, 1, d), F32), *lru_w, tm=tm1, stride=1, tiles_per_seq=seq // tm1, alpha=alpha)
    conv_c_prompt = tailp.reshape(nb, seq // tm1, SUBLANES, d)[:, -1, SUBLANES - (conv_k - 1):][None]
    h_prompt = hp.reshape(nb, seq // tm1, d)[:, -1][None]

    state_c = to_time_major(state_conv_c[0])[None]
    ys, tails, hs = _lru(ys, shift_s, scale_s, gate_s, state_c, state_h[0][None], *lru_w, tm=ns, stride=ns,
                         tiles_per_seq=dec, alpha=alpha)
    conv_c_sample = to_seq_major(tails[-1], conv_k - 1)[None]
    h_sample = hs[-1][None]

    y_prompt = yp.reshape(nb, seq, d)
    y_sample = to_seq_major(ys, dec)
    k_prompt = kp.reshape(1, nb, seq, n_heads, HEAD_DIM)
    v_prompt = vp.reshape(1, nb, seq, n_heads, HEAD_DIM)
    return (y_prompt, y_sample, k_prompt, v_prompt, conv_b_prompt, conv_c_prompt, h_prompt,
            k_sample, v_sample, conv_b_sample, conv_c_sample, h_sample)
```

```python
import functools
import math

import jax
import jax.numpy as jnp
from jax import lax
from jax.experimental import pallas as pl
from jax.experimental.pallas import tpu as pltpu

F32 = jnp.float32
BF16 = jnp.bfloat16

HEAD_DIM = 128
SB_BLOCK = 256
LRU_C = 8.0
LN_EPS = 1e-5
LOG2E = math.log2(math.e)
Q_SCALE = LOG2E / math.sqrt(HEAD_DIM)
SUBLANES = 8
VMEM_LIMIT = 58 * 1024 * 1024


def _cparams(n_axes):
    return pltpu.CompilerParams(dimension_semantics=("arbitrary",) * n_axes, vmem_limit_bytes=VMEM_LIMIT)


def _silu(x):
    return x * jax.nn.sigmoid(x)


def _idiv(x, n):
    assert n & (n - 1) == 0
    return x >> (n.bit_length() - 1)


def _imod(x, n):
    assert n & (n - 1) == 0
    return x & (n - 1)


def _dot(a, b):
    return jnp.dot(a, b, preferred_element_type=F32)


def _dot_nt(a, b):
    return lax.dot_general(a, b, (((1,), (1,)), ((), ())), preferred_element_type=F32)


def _modulate(x, scale, shift):
    gm = scale.shape[0]
    if gm == 1:
        return x * (1.0 + scale) + shift
    rows, d = x.shape
    x3 = x.reshape(rows // gm, gm, d)
    return (x3 * (1.0 + scale)[None] + shift[None]).reshape(rows, d)


def _gate_mix(mix, gate):
    gm = gate.shape[0]
    if gm == 1:
        return mix * (1.0 + gate)
    rows, d = mix.shape
    return (mix.reshape(rows // gm, gm, d) * (1.0 + gate)[None]).reshape(rows, d)


def _shift_rows(cur, prev, d):
    tm, p = cur.shape[0], prev.shape[0]
    if d % SUBLANES == 0:
        if d >= tm:
            return prev[p - d:p - d + tm]
        return jnp.concatenate([prev[p - d:], cur[:tm - d]], axis=0)
    assert d < SUBLANES <= p
    rolled = pltpu.roll(cur, d, 0)
    prev_rolled = pltpu.roll(prev[p - SUBLANES:], d, 0)
    row = lax.broadcasted_iota(jnp.int32, prev_rolled.shape, 0)
    head = jnp.where(row < d, prev_rolled, rolled[:SUBLANES])
    return jnp.concatenate([head, rolled[SUBLANES:]], axis=0)


def _next_prev(cur, prev):
    tm, p = cur.shape[0], prev.shape[0]
    if tm >= p:
        return cur[tm - p:]
    return jnp.concatenate([prev[tm:], cur], axis=0)


def _ada_kernel(c_ref, w0_ref, w1_ref, b0_ref, b1_ref, o_ref):
    layer = pl.program_id(0)
    k = pl.program_id(1)
    s = _silu(c_ref[...]).astype(BF16)
    for this, w_ref, b_ref in ((0, w0_ref, b0_ref), (1, w1_ref, b1_ref)):
        @pl.when((layer == this) & (k == 0))
        def _():
            o_ref[...] = _dot(s, w_ref[...].astype(BF16)) + b_ref[...]

        @pl.when((layer == this) & (k > 0))
        def _():
            o_ref[...] += _dot(s, w_ref[...].astype(BF16))


def _ada(c_all, w0, b0, w1, b1, tk=256):
    rows, d = c_all.shape
    n = w0.shape[1]
    nk = d // tk
    return pl.pallas_call(
        _ada_kernel,
        grid=(2, nk),
        in_specs=[pl.BlockSpec((rows, tk), lambda l, k: (0, k)),
                  pl.BlockSpec((tk, n), lambda l, k: (k * (1 - l) + (nk - 1) * l, 0)),
                  pl.BlockSpec((tk, n), lambda l, k: (k * l, 0)),
                  pl.BlockSpec((1, n), lambda l, k: (0, 0)),
                  pl.BlockSpec((1, n), lambda l, k: (0, 0))],
        out_specs=pl.BlockSpec((None, rows, n), lambda l, k: (l, 0, 0)),
        out_shape=jax.ShapeDtypeStruct((2, rows, n), F32),
        compiler_params=_cparams(2),
        name="ada",
    )(c_all, w0, w1, b0.reshape(1, n), b1.reshape(1, n))


def _inproj0_kernel(x_ref, shift_ref, scale_ref, state_ref, wq, wk, wv, wza, wbg, wcg, wxi, wzb, wconv_ref,
                    q_ref, k_ref, v_ref, za_ref, ob_ref, tail_ref, u_scr, carry_scr, *, stride, tiles_per_seq):
    i = pl.program_id(0)
    j = pl.program_id(1)

    @pl.when(j == 0)
    def _():
        u_scr[...] = _modulate(x_ref[...], scale_ref[...], shift_ref[...]).astype(BF16)

    @pl.when(i % tiles_per_seq == 0)
    def _():
        carry_scr[j] = state_ref[...]

    u = u_scr[...]
    q_ref[...] = (_dot(u, wq[...]) * Q_SCALE).astype(q_ref.dtype)
    k_ref[...] = _dot(u, wk[...])
    v_ref[...] = _dot(u, wv[...])
    za_ref[...] = _dot(u, wza[...])
    ci = _dot(u, wcg[...]) * _dot(u, wxi[...])
    prev = carry_scr[j]
    wc = wconv_ref[...]
    conv = wc[2:3] * ci + wc[1:2] * _shift_rows(ci, prev, stride) + wc[0:1] * _shift_rows(ci, prev, 2 * stride)
    ob_ref[...] = (_dot(u, wbg[...]) * conv * _silu(_dot(u, wzb[...]))).astype(ob_ref.dtype)
    new_prev = _next_prev(ci, prev)
    carry_scr[j] = new_prev
    tail_ref[...] = new_prev


def _inproj0(x, shift, scale, state, w_in, w_conv, *, tm, tn, stride, tiles_per_seq, q_dtype):
    rows, d = x.shape
    c = w_conv.shape[1]
    p = state.shape[1]
    n_i, n_j = rows // tm, c // tn
    gm = shift.shape[-2]
    if shift.ndim == 3:
        mod_spec = pl.BlockSpec((None, gm, d), lambda i, j: (i // tiles_per_seq, 0, 0))
    else:
        mod_spec = pl.BlockSpec((gm, d), lambda i, j: (0, 0))

    def w_spec(g):
        return pl.BlockSpec((d, tn), lambda i, j: (0, g * n_j + j))

    act_spec = pl.BlockSpec((tm, tn), lambda i, j: (i, j))
    outs = pl.pallas_call(
        functools.partial(_inproj0_kernel, stride=stride, tiles_per_seq=tiles_per_seq),
        grid=(n_i, n_j),
        in_specs=[pl.BlockSpec((tm, d), lambda i, j: (i, 0)), mod_spec, mod_spec,
                  pl.BlockSpec((None, p, tn), lambda i, j: (i // tiles_per_seq, 0, j))]
                 + [w_spec(g) for g in range(8)]
                 + [pl.BlockSpec((w_conv.shape[0], tn), lambda i, j: (0, j))],
        out_specs=[act_spec] * 5 + [pl.BlockSpec((None, p, tn), lambda i, j: (i, 0, j))],
        out_shape=[jax.ShapeDtypeStruct((rows, c), q_dtype)] + [jax.ShapeDtypeStruct((rows, c), F32)] * 3
                  + [jax.ShapeDtypeStruct((rows, c), BF16), jax.ShapeDtypeStruct((n_i, p, c), F32)],
        scratch_shapes=[pltpu.VMEM((tm, d), BF16), pltpu.VMEM((n_j, p, tn), F32)],
        compiler_params=_cparams(2),
        name="inproj0",
    )(x, shift, scale, state, *([w_in] * 8), w_conv)
    return outs


def _sb_softplus(z2, mask):
    neg_abs = lax.bitcast_convert_type(lax.bitcast_convert_type(z2, jnp.uint32) | jnp.uint32(0x80000000), F32)
    sp = jnp.maximum(z2, 0.0) + jnp.log2(1.0 + jnp.exp2(neg_abs))
    return sp if mask is None else jnp.where(mask, sp, 0.0)


def _sb_exp(z2, incl, carry, mask):
    w = jnp.exp2(z2 - incl - carry)
    return (w if mask is None else jnp.where(mask, w, 0.0)).astype(BF16)


def _tri_strict(t):
    r = lax.broadcasted_iota(jnp.int32, (t, t), 0)
    c = lax.broadcasted_iota(jnp.int32, (t, t), 1)
    return jnp.where(r > c, 1.0, 0.0).astype(BF16)


def _suffix_sum_heads(x, n_heads):
    n = x.shape[1]
    lane = lax.broadcasted_iota(jnp.int32, x.shape, 1)
    s = n_heads
    while s < n:
        x = x + jnp.where(lane < n - s, pltpu.roll(x, n - s, 1), 0.0)
        s *= 2
    return x


def _bcast_heads(v, n_heads):
    lane = lax.broadcasted_iota(jnp.int32, v.shape, 1)
    v = jnp.where(lane < n_heads, v, 0.0)
    s = n_heads
    while s < v.shape[1]:
        v = v + pltpu.roll(v, s, 1)
        s *= 2
    return v


class _SampleAttn:
    def __init__(self, bias_ref, q_ref, kn_ref, vn_ref, k_pages, v_pages, n_heads):
        self.n_heads = n_heads
        self.t_new = t_new = q_ref.shape[0]
        self.k_pages, self.v_pages, self.kn_ref, self.vn_ref = k_pages, v_pages, kn_ref, vn_ref
        n_pages = len(k_pages)
        rows_pp = k_pages[0].shape[0]
        self.blocks = [n_pages] + list(range(n_pages - 1, -1, -1))
        self.sizes = [HEAD_DIM if j == n_pages else rows_pp for j in self.blocks]
        q = q_ref[...]
        self.q_rows = jnp.concatenate([q[:, h * HEAD_DIM:(h + 1) * HEAD_DIM] for h in range(n_heads)],
                                      axis=0).astype(BF16)
        row = lax.broadcasted_iota(jnp.int32, (t_new, HEAD_DIM), 0)
        lanes = {n: lax.broadcasted_iota(jnp.int32, (t_new, n), 1) for n in set(self.sizes)}
        self.heads_of = {n: _imod(lanes[n], n_heads) for n in set(self.sizes)}
        self.biases = {}
        for n in set(self.sizes):
            bias = jnp.full((1, n), bias_ref[0, 0] * LOG2E, F32)
            for h in range(1, n_heads):
                bias = jnp.where(self.heads_of[n][0:1] == h, bias_ref[0, h] * LOG2E, bias)
            self.biases[n] = bias
        new_mask = _idiv(lanes[HEAD_DIM], n_heads) < row
        self.masks = [new_mask if j == n_pages else None for j in self.blocks]

    def _block(self, pages, new_ref, j):
        if j < len(pages):
            return pages[j][...].astype(BF16)
        pad = jnp.zeros((HEAD_DIM - new_ref.shape[0], HEAD_DIM), F32)
        return jnp.concatenate([new_ref[...], pad], axis=0).astype(BF16)

    def logits(self):
        t_new = self.t_new
        self.z2s = []
        for j, n in zip(self.blocks, self.sizes):
            s = _dot_nt(self.q_rows, self._block(self.k_pages, self.kn_ref, j))
            z2 = s[0:t_new]
            for h in range(1, self.n_heads):
                z2 = jnp.where(self.heads_of[n] == h, s[h * t_new:(h + 1) * t_new], z2)
            self.z2s.append(z2 + self.biases[n])

    def weights(self):
        sps = [_sb_softplus(z2, m) for z2, m in zip(self.z2s, self.masks)]
        incls = [_suffix_sum_heads(sp, self.n_heads) for sp in sps]
        totals = [_bcast_heads(incl[:, 0:HEAD_DIM], self.n_heads) for incl in incls]
        carry = jnp.zeros((self.t_new, HEAD_DIM), F32)
        self.w_rows = []
        for i, n in enumerate(self.sizes):
            w = jnp.exp2(self.z2s[i] - incls[i] - jnp.concatenate([carry] * (n // HEAD_DIM), axis=1))
            if self.masks[i] is not None:
                w = jnp.where(self.masks[i], w, 0.0)
            self.w_rows.append(jnp.concatenate([jnp.where(self.heads_of[n] == h, w, 0.0)
                                                for h in range(self.n_heads)], axis=0).astype(BF16))
            carry = carry + totals[i]

    def output(self, za_ref, o_ref):
        t_new = self.t_new
        acc = jnp.zeros((self.n_heads * t_new, HEAD_DIM), F32)
        for j, wr in zip(self.blocks, self.w_rows):
            acc = acc + _dot(wr, self._block(self.v_pages, self.vn_ref, j))
        o = jnp.concatenate([acc[h * t_new:(h + 1) * t_new] for h in range(self.n_heads)], axis=1)
        o_ref[...] = (o * _silu(za_ref[...])).astype(o_ref.dtype)


def _pick(cond, a, b):
    if isinstance(cond, bool):
        return a if cond else b
    return jnp.where(cond, a, b)


class _PromptAttn:
    def __init__(self, bias, q_ref, k_ref, v_ref, s, t):
        self.q_ref, self.k_ref, self.v_ref, self.s, self.t = q_ref, k_ref, v_ref, s, t
        self.nq = nq = q_ref.shape[0] // t
        self.n_items = nq + 1
        self.bias = bias * LOG2E
        self.in_a = [True if i == 0 else False if i >= nq // 2 else i <= s for i in range(self.n_items)]
        self.qrow = [pl.multiple_of(_pick(self.in_a[i], s, nq - 1 - s) * t, t) for i in range(self.n_items)]
        self.krow = [pl.multiple_of(_pick(self.in_a[i], s - i, nq - i) * t, t) for i in range(self.n_items)]
        r = lax.broadcasted_iota(jnp.int32, (t, t), 0)
        c = lax.broadcasted_iota(jnp.int32, (t, t), 1)
        self.masks = []
        for i in range(self.n_items):
            if i == 0:
                self.masks.append(c < r)
            elif i <= nq // 2:
                self.masks.append(c < r + jnp.where(i == s + 1, 0, t))
            else:
                self.masks.append(None)

    def logits(self):
        t = self.t
        self.zs = [_dot_nt(self.q_ref[pl.ds(self.qrow[i], t), :], self.k_ref[pl.ds(self.krow[i], t), :].astype(BF16))
                   + self.bias for i in range(self.n_items)]

    def weights(self):
        t = self.t
        tri = _tri_strict(t)
        sps = [_sb_softplus(z2, m) for z2, m in zip(self.zs, self.masks)]
        incls = [sp + _dot(sp.astype(BF16), tri) for sp in sps]
        run_a = jnp.zeros((t, 1), F32)
        run_b = jnp.zeros((t, 1), F32)
        self.ws = []
        for i in range(self.n_items):
            rowsum = jnp.sum(sps[i], axis=-1, keepdims=True)
            carry = _pick(self.in_a[i], run_a, run_b)
            self.ws.append(_sb_exp(self.zs[i], incls[i], carry, self.masks[i]))
            run_a = run_a + _pick(self.in_a[i], rowsum, 0.0)
            run_b = run_b + _pick(self.in_a[i], 0.0, rowsum)

    def output(self, za_ref, o_ref):
        t, s = self.t, self.s
        acc_a = jnp.zeros((t, HEAD_DIM), F32)
        acc_b = jnp.zeros((t, HEAD_DIM), F32)
        for i in range(self.n_items):
            pv = _dot(self.ws[i], self.v_ref[pl.ds(self.krow[i], t), :].astype(BF16))
            acc_a = acc_a + _pick(self.in_a[i], pv, 0.0)
            acc_b = acc_b + _pick(self.in_a[i], 0.0, pv)
        for acc, tile in ((acc_a, s), (acc_b, self.nq - 1 - s)):
            rows = pl.ds(pl.multiple_of(tile * t, t), t)
            o_ref[rows, :] = (acc * _silu(za_ref[rows, :])).astype(o_ref.dtype)


def _attn_kernel(pt_ref, bias_ref, q_ref, kn_ref, vn_ref, za_ref, *rest, n_pages, n_heads, t):
    del pt_ref
    k_pages = rest[:n_pages]
    v_pages = rest[n_pages:2 * n_pages]
    qp_ref, kp_ref, vp_ref, zap_ref, o_ref, op_ref = rest[2 * n_pages:]
    nq = qp_ref.shape[0] // t
    n = pl.program_id(0)
    pairs = nq // 2
    head = _imod(_idiv(n, pairs), n_heads)
    sample = _SampleAttn(bias_ref, q_ref, kn_ref, vn_ref, k_pages, v_pages, n_heads)
    prompt = _PromptAttn(bias_ref[0, head], qp_ref, kp_ref, vp_ref, _imod(n, pairs), t)
    sample.logits()
    prompt.logits()
    sample.weights()
    prompt.weights()
    sample.output(za_ref, o_ref)
    prompt.output(zap_ref, op_ref)


def _attn(page_table, bias, q, k_new, v_new, za, cache_k, cache_v, qp, kp, vp, zap, *, n_phys, n_batch, seq):
    n_seq, t_new, width = q.shape
    n_pages = page_table.shape[1]
    n_heads = width // HEAD_DIM
    page = cache_k.shape[0] // (n_phys * n_heads)
    t = SB_BLOCK
    pairs = seq // t // 2
    assert n_seq == n_batch * n_heads * pairs, "one (prompt sequence, head, query-tile pair) per sample sequence"
    seq_spec = pl.BlockSpec((None, t_new, width), lambda n, pt: (n, 0, 0))
    new_spec = pl.BlockSpec((None, t_new * n_heads, HEAD_DIM), lambda n, pt: (n, 0, 0))
    head_spec = pl.BlockSpec((seq, HEAD_DIM), lambda n, pt: (n // (n_heads * pairs), (n // pairs) % n_heads))

    def page_spec(p):
        return pl.BlockSpec((page * n_heads, HEAD_DIM), lambda n, pt: (pt[n, p], 0))

    grid_spec = pltpu.PrefetchScalarGridSpec(
        num_scalar_prefetch=1,
        grid=(n_seq,),
        in_specs=[pl.BlockSpec(memory_space=pltpu.SMEM), seq_spec, new_spec, new_spec, seq_spec]
                 + [page_spec(p) for p in range(n_pages)] * 2 + [head_spec] * 4,
        out_specs=[seq_spec, head_spec],
    )
    return pl.pallas_call(
        functools.partial(_attn_kernel, n_pages=n_pages, n_heads=n_heads, t=t),
        grid_spec=grid_spec,
        out_shape=[jax.ShapeDtypeStruct((n_seq, t_new, width), F32), jax.ShapeDtypeStruct(qp.shape, BF16)],
        compiler_params=_cparams(1),
        name="attn",
    )(page_table, bias, q, k_new, v_new, za, *([cache_k] * n_pages), *([cache_v] * n_pages), qp, kp, vp, zap)


def _residual_ln(x, mix, gate, g, b, alpha):
    r = alpha * x + _gate_mix(mix, gate)
    mu = jnp.mean(r, axis=-1, keepdims=True)
    rc = r - mu
    var = jnp.mean(rc * rc, axis=-1, keepdims=True)
    return rc * lax.rsqrt(var + LN_EPS) * g + b


def _outproj_ln_kernel(*refs, n_in, n_cast, alpha):
    x_ref, gate_ref = refs[0], refs[1]
    acts = refs[2:2 + n_in]
    ws = refs[2 + n_in:2 + 2 * n_in]
    g_ref, b_ref = refs[2 + 2 * n_in:4 + 2 * n_in]
    cast_src = refs[4 + 2 * n_in:4 + 2 * n_in + n_cast]
    y_ref = refs[4 + 2 * n_in + n_cast]
    cast_dst = refs[5 + 2 * n_in + n_cast:]
    mix = _dot(acts[0][...].astype(BF16), ws[0][...])
    for a, w in zip(acts[1:], ws[1:]):
        mix = mix + _dot(a[...].astype(BF16), w[...])
    y_ref[...] = _residual_ln(x_ref[...], mix, gate_ref[...], g_ref[...], b_ref[...], alpha)
    for src, dst in zip(cast_src, cast_dst):
        dst[...] = src[...].astype(dst.dtype)


def _outproj_ln(x, gate, acts, w_out, ln_g, ln_b, *, tm, tiles_per_seq, alpha, casts=()):
    rows, d = x.shape
    n_steps = rows // tm
    gm = gate.shape[-2]
    if gate.ndim == 3:
        mod_spec = pl.BlockSpec((None, gm, d), lambda i: (i // tiles_per_seq, 0, 0))
    else:
        mod_spec = pl.BlockSpec((gm, d), lambda i: (0, 0))
    n_in = len(acts)
    kw = w_out.shape[0] // n_in
    once = pl.Buffered(1)
    w_specs = [pl.BlockSpec((kw, d), functools.partial(lambda i, g: (g, 0), g=g), pipeline_mode=once)
               for g in range(n_in)]
    vec_spec = pl.BlockSpec((1, d), lambda i: (0, 0))
    cast_specs = [pl.BlockSpec((c.shape[0] // n_steps, c.shape[1]), lambda i: (i, 0)) for c in casts]
    outs = pl.pallas_call(
        functools.partial(_outproj_ln_kernel, n_in=n_in, n_cast=len(casts), alpha=alpha),
        grid=(n_steps,),
        in_specs=[pl.BlockSpec((tm, d), lambda i: (i, 0)), mod_spec]
                 + [pl.BlockSpec((tm, kw), lambda i: (i, 0))] * n_in + w_specs + [vec_spec, vec_spec] + cast_specs,
        out_specs=[pl.BlockSpec((tm, d), lambda i: (i, 0))] + cast_specs,
        out_shape=[jax.ShapeDtypeStruct((rows, d), F32)] + [jax.ShapeDtypeStruct(c.shape, BF16) for c in casts],
        compiler_params=_cparams(1),
        name="outproj_ln",
    )(x, gate, *acts, *([w_out] * n_in), ln_g.reshape(1, d), ln_b.reshape(1, d), *casts)
    return outs[0] if not casts else tuple(outs)


def _scan_rows(a, b, h_scr, h0):
    rows, d = a.shape
    g = rows // SUBLANES
    a = a.reshape(g, SUBLANES, d)
    b = b.reshape(g, SUBLANES, d)
    row = lax.broadcasted_iota(jnp.int32, (1, SUBLANES, d), 1)
    s = 1
    while s < SUBLANES:
        keep = row >= s
        b = a * jnp.where(keep, pltpu.roll(b, s, 1), 0.0) + b
        a = a * jnp.where(keep, pltpu.roll(a, s, 1), 1.0)
        s *= 2
    h = h0
    for i in range(g):
        hg = a[i] * h + b[i]
        h_scr[i * SUBLANES:(i + 1) * SUBLANES, :] = hg
        h = hg[SUBLANES - 1:]
    return h


def _lru_kernel(x_ref, shift_ref, scale_ref, gate_ref, state_ref, h0_ref, wx_ref, wz_ref, wconv_ref, bconv_ref,
                wga_ref, wgx_ref, bga_ref, bgx_ref, lam_ref, wout_ref, lng_ref, lnb_ref,
                y_ref, tail_ref, hlast_ref, prev_scr, hcar_scr, h_scr, *, stride, tiles_per_seq, alpha):
    i = pl.program_id(0)

    @pl.when(i % tiles_per_seq == 0)
    def _():
        prev_scr[...] = state_ref[...]
        hcar_scr[...] = h0_ref[...]

    u = _modulate(x_ref[...], scale_ref[...], shift_ref[...]).astype(BF16)
    xr = _dot(u, wx_ref[...])
    prev = prev_scr[...]
    wc = wconv_ref[...]
    xc = wc[3:4] * xr + bconv_ref[...]
    for kk in range(1, 4):
        xc = xc + wc[3 - kk:4 - kk] * _shift_rows(xr, prev, kk * stride)
    new_prev = _next_prev(xr, prev)
    prev_scr[...] = new_prev
    tail_ref[...] = new_prev

    xcb = xc.astype(BF16)
    n_blocks, blk, _ = wga_ref.shape
    ga = jnp.concatenate([_dot(xcb[:, n * blk:(n + 1) * blk], wga_ref[n]) for n in range(n_blocks)], axis=1)
    gx = jnp.concatenate([_dot(xcb[:, n * blk:(n + 1) * blk], wgx_ref[n]) for n in range(n_blocks)], axis=1)
    lam = lam_ref[...]
    neg_log_sig = jnp.maximum(-lam, 0.0) + jnp.log1p(jnp.exp(-jnp.abs(lam)))
    log_a = (-LRU_C) * jax.nn.sigmoid(ga + bga_ref[...]) * neg_log_sig
    a = jnp.exp(log_a)
    bt = jnp.sqrt(jnp.tanh(-log_a) * (1.0 + a * a)) * jax.nn.sigmoid(gx + bgx_ref[...]) * xc

    if stride % SUBLANES == 0:
        h_t = hcar_scr[...]
        slabs = []
        for r0 in range(0, x_ref.shape[0], stride):
            h_t = a[r0:r0 + stride] * h_t + bt[r0:r0 + stride]
            slabs.append(h_t)
        h = jnp.concatenate(slabs, axis=0)
        hcar_scr[...] = h_t
    else:
        hcar_scr[...] = _scan_rows(a, bt, h_scr, hcar_scr[...])
        h = h_scr[...]
    hlast_ref[...] = hcar_scr[...]
    z = _dot(u, wz_ref[...])
    mix = _dot((h * _silu(z)).astype(BF16), wout_ref[...])
    y_ref[...] = _residual_ln(x_ref[...], mix, gate_ref[...], lng_ref[...], lnb_ref[...], alpha)


def _lru(x, shift, scale, gate, state, h0, w_in, w_conv, b_conv, w_ga, w_gx, b_ga, b_gx, lam, w_out, ln_g, ln_b, *,
         tm, stride, tiles_per_seq, alpha):
    rows, d = x.shape
    p = state.shape[1]
    hr = h0.shape[1]
    n_i = rows // tm
    gm = shift.shape[-2]
    if shift.ndim == 3:
        mod_spec = pl.BlockSpec((None, gm, d), lambda i: (i // tiles_per_seq, 0, 0))
    else:
        mod_spec = pl.BlockSpec((gm, d), lambda i: (0, 0))
    once = pl.Buffered(1)
    vec_spec = pl.BlockSpec((1, d), lambda i: (0, 0))
    gate_spec = pl.BlockSpec(w_ga.shape, lambda i: (0, 0, 0), pipeline_mode=once)
    scan_scr = (tm, d) if stride % SUBLANES else (SUBLANES, 128)
    return pl.pallas_call(
        functools.partial(_lru_kernel, stride=stride, tiles_per_seq=tiles_per_seq, alpha=alpha),
        grid=(n_i,),
        in_specs=[pl.BlockSpec((tm, d), lambda i: (i, 0)), mod_spec, mod_spec, mod_spec,
                  pl.BlockSpec((None, p, d), lambda i: (i // tiles_per_seq, 0, 0)),
                  pl.BlockSpec((None, hr, d), lambda i: (i // tiles_per_seq, 0, 0)),
                  pl.BlockSpec((d, d), lambda i: (0, 0), pipeline_mode=once),
                  pl.BlockSpec((d, d), lambda i: (0, 1), pipeline_mode=once),
                  pl.BlockSpec((w_conv.shape[0], d), lambda i: (0, 0)), vec_spec,
                  gate_spec, gate_spec, vec_spec, vec_spec, vec_spec,
                  pl.BlockSpec((d, d), lambda i: (0, 0), pipeline_mode=once), vec_spec, vec_spec],
        out_specs=[pl.BlockSpec((tm, d), lambda i: (i, 0)),
                   pl.BlockSpec((None, p, d), lambda i: (i, 0, 0)),
                   pl.BlockSpec((None, hr, d), lambda i: (i, 0, 0))],
        out_shape=[jax.ShapeDtypeStruct((rows, d), F32), jax.ShapeDtypeStruct((n_i, p, d), F32),
                   jax.ShapeDtypeStruct((n_i, hr, d), F32)],
        scratch_shapes=[pltpu.VMEM((p, d), F32), pltpu.VMEM((hr, d), F32), pltpu.VMEM(scan_scr, F32)],
        compiler_params=_cparams(1),
        name="lru",
    )(x, shift, scale, gate, state, h0, w_in, w_in, w_conv, b_conv.reshape(1, d), w_ga, w_gx,
      b_ga.reshape(1, d), b_gx.reshape(1, d), lam.reshape(1, d), w_out, ln_g.reshape(1, d), ln_b.reshape(1, d))


def kernel(x_prompt, x_sample, c_prompt, c_sample, cache_k, cache_v, state_conv_b, state_conv_c, state_h, page_table, we_ada, be_ada, we_in, we_sb_bias, we_conv, we_out, ge_ln, be_ln, wo_ada, bo_ada, wo_in, wo_conv, bo_conv, wo_gate_a, bo_gate_a, wo_gate_x, bo_gate_x, wo_lambda, wo_out, go_ln, bo_ln):
    nb, seq, d = x_prompt.shape
    ns, dec, _ = x_sample.shape
    assert we_ada.shape[0] == 1 and wo_ada.shape[0] == 1, "one even and one odd layer"
    n_heads = we_sb_bias.shape[1]
    width = n_heads * HEAD_DIM
    alpha = (2.0 * (we_ada.shape[0] + wo_ada.shape[0])) ** 0.25
    tm = 512
    tm_in0 = 1024

    def to_time_major(a):
        return jnp.swapaxes(a, 0, 1).reshape(a.shape[1] * ns, a.shape[2])

    def to_seq_major(a, t):
        return jnp.swapaxes(a.reshape(t, ns, a.shape[-1]), 0, 1)

    pad = (-(nb + ns)) % SUBLANES
    c_all = jnp.concatenate([c_prompt, c_sample, jnp.zeros((pad, d), F32)], axis=0)
    mods = []
    m_all = _ada(c_all, we_ada[0], be_ada[0], wo_ada[0], bo_ada[0])
    for m in (m_all[0], m_all[1]):
        parts = [m[:, g * d:(g + 1) * d] for g in range(3)]
        mods.append(([v[:nb].reshape(nb, 1, d) for v in parts], [v[nb:nb + ns] for v in parts]))

    xp = x_prompt.reshape(nb * seq, d)
    xs = to_time_major(x_sample)
    bias = we_sb_bias[0].reshape(1, n_heads)

    (shift_p, scale_p, gate_p), (shift_s, scale_s, gate_s) = mods[0]
    w_in0 = we_in[0].astype(BF16)
    w_out0 = we_out[0].astype(BF16)
    conv_k = we_conv.shape[1]
    qp, kp, vp, zap, obp, tailp = _inproj0(
        xp, shift_p, scale_p, jnp.zeros((nb, SUBLANES, width), F32), w_in0, we_conv[0],
        tm=tm_in0, tn=256, stride=1, tiles_per_seq=seq // tm_in0, q_dtype=BF16)
    conv_b_prompt = tailp.reshape(nb, seq // tm_in0, SUBLANES, width)[:, -1, SUBLANES - (conv_k - 1):][None]

    state_b = to_time_major(state_conv_b[0])[None]
    rows_s = ns * dec
    tiles_s = rows_s // tm
    qs, ks, vs, zas, obs, tails = _inproj0(
        xs, shift_s, scale_s, state_b, w_in0, we_conv[0],
        tm=tm_in0, tn=256, stride=ns, tiles_per_seq=rows_s // tm_in0, q_dtype=F32)
    k_sample = to_seq_major(ks, dec).reshape(1, ns, dec, n_heads, HEAD_DIM)
    v_sample = to_seq_major(vs, dec).reshape(1, ns, dec, n_heads, HEAD_DIM)
    n_phys = cache_k.shape[1]
    oas, oap = _attn(page_table, bias, to_seq_major(qs, dec), k_sample.reshape(ns, dec * n_heads, HEAD_DIM),
                     v_sample.reshape(ns, dec * n_heads, HEAD_DIM), to_seq_major(zas, dec),
                     cache_k.reshape(-1, HEAD_DIM), cache_v.reshape(-1, HEAD_DIM), qp, kp, vp, zap,
                     n_phys=n_phys, n_batch=nb, seq=seq)
    yp, w_in1, w_out1 = _outproj_ln(xp, gate_p, [oap, obp], w_out0, ge_ln[0], be_ln[0], tm=tm,
                                    tiles_per_seq=seq // tm, alpha=alpha, casts=(wo_in[0], wo_out[0]))
    ys = _outproj_ln(xs, gate_s, [to_time_major(oas), obs], w_out0, ge_ln[0], be_ln[0], tm=tm,
                     tiles_per_seq=tiles_s, alpha=alpha)
    conv_b_sample = to_seq_major(tails[-1], conv_k - 1)[None]

    (shift_p, scale_p, gate_p), (shift_s, scale_s, gate_s) = mods[1]
    w_ga = wo_gate_a[0].astype(BF16)
    w_gx = wo_gate_x[0].astype(BF16)
    conv_k = wo_conv.shape[1]
    lru_w = (w_in1, wo_conv[0], bo_conv[0], w_ga, w_gx, bo_gate_a[0], bo_gate_x[0], wo_lambda[0],
             w_out1, go_ln[0], bo_ln[0])
    tm1 = 256
    yp, tailp, hp = _lru(yp, shift_p, scale_p, gate_p, jnp.zeros((nb, SUBLANES, d), F32),
                         jnp.zeros((nb, 1, d), F32), *lru_w, tm=tm1, stride=1, tiles_per_seq=seq // tm1, alpha=alpha)
    conv_c_prompt = tailp.reshape(nb, seq // tm1, SUBLANES, d)[:, -1, SUBLANES - (conv_k - 1):][None]
    h_prompt = hp.reshape(nb, seq // tm1, d)[:, -1][None]

    state_c = to_time_major(state_conv_c[0])[None]
    tm1_s = ns
    ys, tails, hs = _lru(ys, shift_s, scale_s, gate_s, state_c, state_h[0][None], *lru_w, tm=tm1_s, stride=ns,
                         tiles_per_seq=rows_s // tm1_s, alpha=alpha)
    conv_c_sample = to_seq_major(tails[-1], conv_k - 1)[None]
    h_sample = hs[-1][None]

    y_prompt = yp.reshape(nb, seq, d)
    y_sample = to_seq_major(ys, dec)
    k_prompt = kp.reshape(1, nb, seq, n_heads, HEAD_DIM)
    v_prompt = vp.reshape(1, nb, seq, n_heads, HEAD_DIM)
    return (y_prompt, y_sample, k_prompt, v_prompt, conv_b_prompt, conv_c_prompt, h_prompt,
            k_sample, v_sample, conv_b_sample, conv_c_sample, h_sample)
```

```python
import functools
import math

import jax
import jax.numpy as jnp
from jax import lax
from jax.experimental import pallas as pl
from jax.experimental.pallas import tpu as pltpu

F32 = jnp.float32
BF16 = jnp.bfloat16

HEAD_DIM = 128
SB_BLOCK = 256
LRU_C = 8.0
LN_EPS = 1e-5
LOG2E = math.log2(math.e)
Q_SCALE = LOG2E / math.sqrt(HEAD_DIM)
SUBLANES = 8
VMEM_LIMIT = 58 * 1024 * 1024


def _cparams(n_axes):
    return pltpu.CompilerParams(dimension_semantics=("arbitrary",) * n_axes, vmem_limit_bytes=VMEM_LIMIT)


def _silu(x):
    return x * jax.nn.sigmoid(x)


def _idiv(x, n):
    assert n & (n - 1) == 0
    return x >> (n.bit_length() - 1)


def _imod(x, n):
    assert n & (n - 1) == 0
    return x & (n - 1)


def _dot(a, b):
    return jnp.dot(a, b, preferred_element_type=F32)


def _dot_nt(a, b):
    return lax.dot_general(a, b, (((1,), (1,)), ((), ())), preferred_element_type=F32)


def _modulate(x, scale, shift):
    gm = scale.shape[0]
    if gm == 1:
        return x * (1.0 + scale) + shift
    rows, d = x.shape
    x3 = x.reshape(rows // gm, gm, d)
    return (x3 * (1.0 + scale)[None] + shift[None]).reshape(rows, d)


def _gate_mix(mix, gate):
    gm = gate.shape[0]
    if gm == 1:
        return mix * (1.0 + gate)
    rows, d = mix.shape
    return (mix.reshape(rows // gm, gm, d) * (1.0 + gate)[None]).reshape(rows, d)


def _shift_rows(cur, prev, d):
    tm, p = cur.shape[0], prev.shape[0]
    if d % SUBLANES == 0:
        if d >= tm:
            return prev[p - d:p - d + tm]
        return jnp.concatenate([prev[p - d:], cur[:tm - d]], axis=0)
    assert d < SUBLANES <= p
    rolled = pltpu.roll(cur, d, 0)
    prev_rolled = pltpu.roll(prev[p - SUBLANES:], d, 0)
    row = lax.broadcasted_iota(jnp.int32, prev_rolled.shape, 0)
    head = jnp.where(row < d, prev_rolled, rolled[:SUBLANES])
    return jnp.concatenate([head, rolled[SUBLANES:]], axis=0)


def _next_prev(cur, prev):
    tm, p = cur.shape[0], prev.shape[0]
    if tm >= p:
        return cur[tm - p:]
    return jnp.concatenate([prev[tm:], cur], axis=0)


def _ada_kernel(c_ref, w0_ref, w1_ref, b0_ref, b1_ref, o_ref):
    layer = pl.program_id(0)
    k = pl.program_id(1)
    s = _silu(c_ref[...]).astype(BF16)
    for this, w_ref, b_ref in ((0, w0_ref, b0_ref), (1, w1_ref, b1_ref)):
        @pl.when((layer == this) & (k == 0))
        def _():
            o_ref[...] = _dot(s, w_ref[...].astype(BF16)) + b_ref[...]

        @pl.when((layer == this) & (k > 0))
        def _():
            o_ref[...] += _dot(s, w_ref[...].astype(BF16))


def _ada(c_all, w0, b0, w1, b1, tk=256):
    rows, d = c_all.shape
    n = w0.shape[1]
    nk = d // tk
    return pl.pallas_call(
        _ada_kernel,
        grid=(2, nk),
        in_specs=[pl.BlockSpec((rows, tk), lambda l, k: (0, k)),
                  pl.BlockSpec((tk, n), lambda l, k: (k * (1 - l) + (nk - 1) * l, 0)),
                  pl.BlockSpec((tk, n), lambda l, k: (k * l, 0)),
                  pl.BlockSpec((1, n), lambda l, k: (0, 0)),
                  pl.BlockSpec((1, n), lambda l, k: (0, 0))],
        out_specs=pl.BlockSpec((None, rows, n), lambda l, k: (l, 0, 0)),
        out_shape=jax.ShapeDtypeStruct((2, rows, n), F32),
        compiler_params=_cparams(2),
        name="ada",
    )(c_all, w0, w1, b0.reshape(1, n), b1.reshape(1, n))


def _inproj0_kernel(x_ref, shift_ref, scale_ref, state_ref, wq, wk, wv, wza, wbg, wcg, wxi, wzb, wconv_ref,
                    q_ref, k_ref, v_ref, za_ref, ob_ref, tail_ref, u_scr, carry_scr, *, stride, tiles_per_seq):
    i = pl.program_id(0)
    j = pl.program_id(1)

    @pl.when(j == 0)
    def _():
        u_scr[...] = _modulate(x_ref[...], scale_ref[...], shift_ref[...]).astype(BF16)

    @pl.when(i % tiles_per_seq == 0)
    def _():
        carry_scr[j] = state_ref[...]

    u = u_scr[...]
    q_ref[...] = (_dot(u, wq[...]) * Q_SCALE).astype(q_ref.dtype)
    k_ref[...] = _dot(u, wk[...])
    v_ref[...] = _dot(u, wv[...])
    za_ref[...] = _dot(u, wza[...])
    ci = _dot(u, wcg[...]) * _dot(u, wxi[...])
    prev = carry_scr[j]
    wc = wconv_ref[...]
    conv = wc[2:3] * ci + wc[1:2] * _shift_rows(ci, prev, stride) + wc[0:1] * _shift_rows(ci, prev, 2 * stride)
    ob_ref[...] = (_dot(u, wbg[...]) * conv * _silu(_dot(u, wzb[...]))).astype(ob_ref.dtype)
    new_prev = _next_prev(ci, prev)
    carry_scr[j] = new_prev
    tail_ref[...] = new_prev


def _inproj0(x, shift, scale, state, w_in, w_conv, *, tm, tn, stride, tiles_per_seq, q_dtype):
    rows, d = x.shape
    c = w_conv.shape[1]
    p = state.shape[1]
    n_i, n_j = rows // tm, c // tn
    gm = shift.shape[-2]
    if shift.ndim == 3:
        mod_spec = pl.BlockSpec((None, gm, d), lambda i, j: (i // tiles_per_seq, 0, 0))
    else:
        mod_spec = pl.BlockSpec((gm, d), lambda i, j: (0, 0))

    def w_spec(g):
        return pl.BlockSpec((d, tn), lambda i, j: (0, g * n_j + j))

    act_spec = pl.BlockSpec((tm, tn), lambda i, j: (i, j))
    outs = pl.pallas_call(
        functools.partial(_inproj0_kernel, stride=stride, tiles_per_seq=tiles_per_seq),
        grid=(n_i, n_j),
        in_specs=[pl.BlockSpec((tm, d), lambda i, j: (i, 0)), mod_spec, mod_spec,
                  pl.BlockSpec((None, p, tn), lambda i, j: (i // tiles_per_seq, 0, j))]
                 + [w_spec(g) for g in range(8)]
                 + [pl.BlockSpec((w_conv.shape[0], tn), lambda i, j: (0, j))],
        out_specs=[act_spec] * 5 + [pl.BlockSpec((None, p, tn), lambda i, j: (i, 0, j))],
        out_shape=[jax.ShapeDtypeStruct((rows, c), q_dtype)] + [jax.ShapeDtypeStruct((rows, c), F32)] * 3
                  + [jax.ShapeDtypeStruct((rows, c), BF16), jax.ShapeDtypeStruct((n_i, p, c), F32)],
        scratch_shapes=[pltpu.VMEM((tm, d), BF16), pltpu.VMEM((n_j, p, tn), F32)],
        compiler_params=_cparams(2),
        name="inproj0",
    )(x, shift, scale, state, *([w_in] * 8), w_conv)
    return outs


def _sb_softplus(z2, mask):
    neg_abs = lax.bitcast_convert_type(lax.bitcast_convert_type(z2, jnp.uint32) | jnp.uint32(0x80000000), F32)
    sp = jnp.maximum(z2, 0.0) + jnp.log2(1.0 + jnp.exp2(neg_abs))
    return sp if mask is None else jnp.where(mask, sp, 0.0)


def _sb_exp(z2, incl, carry, mask):
    w = jnp.exp2(z2 - incl - carry)
    return (w if mask is None else jnp.where(mask, w, 0.0)).astype(BF16)


def _tri_strict(t):
    r = lax.broadcasted_iota(jnp.int32, (t, t), 0)
    c = lax.broadcasted_iota(jnp.int32, (t, t), 1)
    return jnp.where(r > c, 1.0, 0.0).astype(BF16)


def _suffix_sum_heads(x, n_heads):
    n = x.shape[1]
    lane = lax.broadcasted_iota(jnp.int32, x.shape, 1)
    s = n_heads
    while s < n:
        x = x + jnp.where(lane < n - s, pltpu.roll(x, n - s, 1), 0.0)
        s *= 2
    return x


def _bcast_heads(v, n_heads):
    lane = lax.broadcasted_iota(jnp.int32, v.shape, 1)
    v = jnp.where(lane < n_heads, v, 0.0)
    s = n_heads
    while s < v.shape[1]:
        v = v + pltpu.roll(v, s, 1)
        s *= 2
    return v


class _SampleAttn:
    def __init__(self, bias_ref, q_ref, kn_ref, vn_ref, k_pages, v_pages, n_heads):
        self.n_heads = n_heads
        self.t_new = t_new = q_ref.shape[0]
        self.k_pages, self.v_pages, self.kn_ref, self.vn_ref = k_pages, v_pages, kn_ref, vn_ref
        n_pages = len(k_pages)
        rows_pp = k_pages[0].shape[0]
        self.blocks = [n_pages] + list(range(n_pages - 1, -1, -1))
        self.sizes = [HEAD_DIM if j == n_pages else rows_pp for j in self.blocks]
        q = q_ref[...]
        self.q_rows = jnp.concatenate([q[:, h * HEAD_DIM:(h + 1) * HEAD_DIM] for h in range(n_heads)],
                                      axis=0).astype(BF16)
        row = lax.broadcasted_iota(jnp.int32, (t_new, HEAD_DIM), 0)
        lanes = {n: lax.broadcasted_iota(jnp.int32, (t_new, n), 1) for n in set(self.sizes)}
        self.heads_of = {n: _imod(lanes[n], n_heads) for n in set(self.sizes)}
        self.biases = {}
        for n in set(self.sizes):
            bias = jnp.full((1, n), bias_ref[0, 0] * LOG2E, F32)
            for h in range(1, n_heads):
                bias = jnp.where(self.heads_of[n][0:1] == h, bias_ref[0, h] * LOG2E, bias)
            self.biases[n] = bias
        new_mask = _idiv(lanes[HEAD_DIM], n_heads) < row
        self.masks = [new_mask if j == n_pages else None for j in self.blocks]

    def _block(self, pages, new_ref, j):
        if j < len(pages):
            return pages[j][...].astype(BF16)
        pad = jnp.zeros((HEAD_DIM - new_ref.shape[0], HEAD_DIM), F32)
        return jnp.concatenate([new_ref[...], pad], axis=0).astype(BF16)

    def logits(self):
        t_new = self.t_new
        self.z2s = []
        for j, n in zip(self.blocks, self.sizes):
            s = _dot_nt(self.q_rows, self._block(self.k_pages, self.kn_ref, j))
            z2 = s[0:t_new]
            for h in range(1, self.n_heads):
                z2 = jnp.where(self.heads_of[n] == h, s[h * t_new:(h + 1) * t_new], z2)
            self.z2s.append(z2 + self.biases[n])

    def weights(self):
        sps = [_sb_softplus(z2, m) for z2, m in zip(self.z2s, self.masks)]
        incls = [_suffix_sum_heads(sp, self.n_heads) for sp in sps]
        totals = [_bcast_heads(incl[:, 0:HEAD_DIM], self.n_heads) for incl in incls]
        carry = jnp.zeros((self.t_new, HEAD_DIM), F32)
        self.w_rows = []
        for i, n in enumerate(self.sizes):
            w = jnp.exp2(self.z2s[i] - incls[i] - jnp.concatenate([carry] * (n // HEAD_DIM), axis=1))
            if self.masks[i] is not None:
                w = jnp.where(self.masks[i], w, 0.0)
            self.w_rows.append(jnp.concatenate([jnp.where(self.heads_of[n] == h, w, 0.0)
                                                for h in range(self.n_heads)], axis=0).astype(BF16))
            carry = carry + totals[i]

    def output(self, za_ref, o_ref):
        t_new = self.t_new
        acc = jnp.zeros((self.n_heads * t_new, HEAD_DIM), F32)
        for j, wr in zip(self.blocks, self.w_rows):
            acc = acc + _dot(wr, self._block(self.v_pages, self.vn_ref, j))
        o = jnp.concatenate([acc[h * t_new:(h + 1) * t_new] for h in range(self.n_heads)], axis=1)
        o_ref[...] = (o * _silu(za_ref[...])).astype(o_ref.dtype)


def _pick(cond, a, b):
    if isinstance(cond, bool):
        return a if cond else b
    return jnp.where(cond, a, b)


class _PromptAttn:
    def __init__(self, bias, q_ref, k_ref, v_ref, s, t):
        self.q_ref, self.k_ref, self.v_ref, self.s, self.t = q_ref, k_ref, v_ref, s, t
        self.nq = nq = q_ref.shape[0] // t
        self.n_items = nq + 1
        self.bias = bias * LOG2E
        self.in_a = [True if i == 0 else False if i >= nq // 2 else i <= s for i in range(self.n_items)]
        self.qrow = [pl.multiple_of(_pick(self.in_a[i], s, nq - 1 - s) * t, t) for i in range(self.n_items)]
        self.krow = [pl.multiple_of(_pick(self.in_a[i], s - i, nq - i) * t, t) for i in range(self.n_items)]
        r = lax.broadcasted_iota(jnp.int32, (t, t), 0)
        c = lax.broadcasted_iota(jnp.int32, (t, t), 1)
        self.masks = []
        for i in range(self.n_items):
            if i == 0:
                self.masks.append(c < r)
            elif i <= nq // 2:
                self.masks.append(c < r + jnp.where(i == s + 1, 0, t))
            else:
                self.masks.append(None)

    def logits(self):
        t = self.t
        self.zs = [_dot_nt(self.q_ref[pl.ds(self.qrow[i], t), :], self.k_ref[pl.ds(self.krow[i], t), :].astype(BF16))
                   + self.bias for i in range(self.n_items)]

    def weights(self):
        t = self.t
        tri = _tri_strict(t)
        sps = [_sb_softplus(z2, m) for z2, m in zip(self.zs, self.masks)]
        incls = [sp + _dot(sp.astype(BF16), tri) for sp in sps]
        run_a = jnp.zeros((t, 1), F32)
        run_b = jnp.zeros((t, 1), F32)
        self.ws = []
        for i in range(self.n_items):
            rowsum = jnp.sum(sps[i], axis=-1, keepdims=True)
            carry = _pick(self.in_a[i], run_a, run_b)
            self.ws.append(_sb_exp(self.zs[i], incls[i], carry, self.masks[i]))
            run_a = run_a + _pick(self.in_a[i], rowsum, 0.0)
            run_b = run_b + _pick(self.in_a[i], 0.0, rowsum)

    def output(self, za_ref, o_ref):
        t, s = self.t, self.s
        acc_a = jnp.zeros((t, HEAD_DIM), F32)
        acc_b = jnp.zeros((t, HEAD_DIM), F32)
        for i in range(self.n_items):
            pv = _dot(self.ws[i], self.v_ref[pl.ds(self.krow[i], t), :].astype(BF16))
            acc_a = acc_a + _pick(self.in_a[i], pv, 0.0)
            acc_b = acc_b + _pick(self.in_a[i], 0.0, pv)
        for acc, tile in ((acc_a, s), (acc_b, self.nq - 1 - s)):
            rows = pl.ds(pl.multiple_of(tile * t, t), t)
            o_ref[rows, :] = (acc * _silu(za_ref[rows, :])).astype(o_ref.dtype)


def _attn_kernel(pt_ref, bias_ref, q_ref, kn_ref, vn_ref, za_ref, *rest, n_pages, n_heads, t):
    del pt_ref
    k_pages = rest[:n_pages]
    v_pages = rest[n_pages:2 * n_pages]
    qp_ref, kp_ref, vp_ref, zap_ref, o_ref, op_ref = rest[2 * n_pages:]
    nq = qp_ref.shape[0] // t
    n = pl.program_id(0)
    pairs = nq // 2
    head = _imod(_idiv(n, pairs), n_heads)
    sample = _SampleAttn(bias_ref, q_ref, kn_ref, vn_ref, k_pages, v_pages, n_heads)
    prompt = _PromptAttn(bias_ref[0, head], qp_ref, kp_ref, vp_ref, _imod(n, pairs), t)
    prompt.logits()
    sample.logits()
    prompt.weights()
    sample.weights()
    prompt.output(zap_ref, op_ref)
    sample.output(za_ref, o_ref)


def _attn(page_table, bias, q, k_new, v_new, za, cache_k, cache_v, qp, kp, vp, zap, *, n_phys, n_batch, seq):
    n_seq, t_new, width = q.shape
    n_pages = page_table.shape[1]
    n_heads = width // HEAD_DIM
    page = cache_k.shape[0] // (n_phys * n_heads)
    t = SB_BLOCK
    pairs = seq // t // 2
    assert n_seq == n_batch * n_heads * pairs, "one (prompt sequence, head, query-tile pair) per sample sequence"
    seq_spec = pl.BlockSpec((None, t_new, width), lambda n, pt: (n, 0, 0))
    new_spec = pl.BlockSpec((None, t_new * n_heads, HEAD_DIM), lambda n, pt: (n, 0, 0))
    head_spec = pl.BlockSpec((seq, HEAD_DIM), lambda n, pt: (n // (n_heads * pairs), (n // pairs) % n_heads))

    def page_spec(p):
        return pl.BlockSpec((page * n_heads, HEAD_DIM), lambda n, pt: (pt[n, p], 0))

    grid_spec = pltpu.PrefetchScalarGridSpec(
        num_scalar_prefetch=1,
        grid=(n_seq,),
        in_specs=[pl.BlockSpec(memory_space=pltpu.SMEM), seq_spec, new_spec, new_spec, seq_spec]
                 + [page_spec(p) for p in range(n_pages)] * 2 + [head_spec] * 4,
        out_specs=[seq_spec, head_spec],
    )
    return pl.pallas_call(
        functools.partial(_attn_kernel, n_pages=n_pages, n_heads=n_heads, t=t),
        grid_spec=grid_spec,
        out_shape=[jax.ShapeDtypeStruct((n_seq, t_new, width), F32), jax.ShapeDtypeStruct(qp.shape, BF16)],
        compiler_params=_cparams(1),
        name="attn",
    )(page_table, bias, q, k_new, v_new, za, *([cache_k] * n_pages), *([cache_v] * n_pages), qp, kp, vp, zap)


def _residual_ln(x, mix, gate, g, b, alpha):
    r = alpha * x + _gate_mix(mix, gate)
    mu = jnp.mean(r, axis=-1, keepdims=True)
    rc = r - mu
    var = jnp.mean(rc * rc, axis=-1, keepdims=True)
    return rc * lax.rsqrt(var + LN_EPS) * g + b


def _outproj_ln_kernel(*refs, n_in, n_cast, alpha):
    x_ref, gate_ref = refs[0], refs[1]
    acts = refs[2:2 + n_in]
    ws = refs[2 + n_in:2 + 2 * n_in]
    g_ref, b_ref = refs[2 + 2 * n_in:4 + 2 * n_in]
    cast_src = refs[4 + 2 * n_in:4 + 2 * n_in + n_cast]
    y_ref = refs[4 + 2 * n_in + n_cast]
    cast_dst = refs[5 + 2 * n_in + n_cast:]
    mix = _dot(acts[0][...].astype(BF16), ws[0][...])
    for a, w in zip(acts[1:], ws[1:]):
        mix = mix + _dot(a[...].astype(BF16), w[...])
    y_ref[...] = _residual_ln(x_ref[...], mix, gate_ref[...], g_ref[...], b_ref[...], alpha)
    for src, dst in zip(cast_src, cast_dst):
        dst[...] = src[...].astype(dst.dtype)


def _outproj_ln(x, gate, acts, w_out, ln_g, ln_b, *, tm, tiles_per_seq, alpha, casts=()):
    rows, d = x.shape
    n_steps = rows // tm
    gm = gate.shape[-2]
    if gate.ndim == 3:
        mod_spec = pl.BlockSpec((None, gm, d), lambda i: (i // tiles_per_seq, 0, 0))
    else:
        mod_spec = pl.BlockSpec((gm, d), lambda i: (0, 0))
    n_in = len(acts)
    kw = w_out.shape[0] // n_in
    once = pl.Buffered(1)
    w_specs = [pl.BlockSpec((kw, d), functools.partial(lambda i, g: (g, 0), g=g), pipeline_mode=once)
               for g in range(n_in)]
    vec_spec = pl.BlockSpec((1, d), lambda i: (0, 0))
    cast_specs = [pl.BlockSpec((c.shape[0] // n_steps, c.shape[1]), lambda i: (i, 0)) for c in casts]
    outs = pl.pallas_call(
        functools.partial(_outproj_ln_kernel, n_in=n_in, n_cast=len(casts), alpha=alpha),
        grid=(n_steps,),
        in_specs=[pl.BlockSpec((tm, d), lambda i: (i, 0)), mod_spec]
                 + [pl.BlockSpec((tm, kw), lambda i: (i, 0))] * n_in + w_specs + [vec_spec, vec_spec] + cast_specs,
        out_specs=[pl.BlockSpec((tm, d), lambda i: (i, 0))] + cast_specs,
        out_shape=[jax.ShapeDtypeStruct((rows, d), F32)] + [jax.ShapeDtypeStruct(c.shape, BF16) for c in casts],
        compiler_params=_cparams(1),
        name="outproj_ln",
    )(x, gate, *acts, *([w_out] * n_in), ln_g.reshape(1, d), ln_b.reshape(1, d), *casts)
    return outs[0] if not casts else tuple(outs)


def _scan_rows(a, b, h_scr, h0):
    rows, d = a.shape
    g = rows // SUBLANES
    a = a.reshape(g, SUBLANES, d)
    b = b.reshape(g, SUBLANES, d)
    row = lax.broadcasted_iota(jnp.int32, (1, SUBLANES, d), 1)
    s = 1
    while s < SUBLANES:
        keep = row >= s
        b = a * jnp.where(keep, pltpu.roll(b, s, 1), 0.0) + b
        a = a * jnp.where(keep, pltpu.roll(a, s, 1), 1.0)
        s *= 2
    h = h0
    for i in range(g):
        hg = a[i] * h + b[i]
        h_scr[i * SUBLANES:(i + 1) * SUBLANES, :] = hg
        h = hg[SUBLANES - 1:]
    return h


def _lru_kernel(x_ref, shift_ref, scale_ref, gate_ref, state_ref, h0_ref, wx_ref, wz_ref, wconv_ref, bconv_ref,
                wga_ref, wgx_ref, bga_ref, bgx_ref, lam_ref, wout_ref, lng_ref, lnb_ref,
                y_ref, tail_ref, hlast_ref, prev_scr, hcar_scr, h_scr, *, stride, tiles_per_seq, alpha):
    i = pl.program_id(0)

    @pl.when(i % tiles_per_seq == 0)
    def _():
        prev_scr[...] = state_ref[...]
        hcar_scr[...] = h0_ref[...]

    u = _modulate(x_ref[...], scale_ref[...], shift_ref[...]).astype(BF16)
    xr = _dot(u, wx_ref[...])
    prev = prev_scr[...]
    wc = wconv_ref[...]
    xc = wc[3:4] * xr + bconv_ref[...]
    for kk in range(1, 4):
        xc = xc + wc[3 - kk:4 - kk] * _shift_rows(xr, prev, kk * stride)
    new_prev = _next_prev(xr, prev)
    prev_scr[...] = new_prev
    tail_ref[...] = new_prev

    xcb = xc.astype(BF16)
    n_blocks, blk, _ = wga_ref.shape
    ga = jnp.concatenate([_dot(xcb[:, n * blk:(n + 1) * blk], wga_ref[n]) for n in range(n_blocks)], axis=1)
    gx = jnp.concatenate([_dot(xcb[:, n * blk:(n + 1) * blk], wgx_ref[n]) for n in range(n_blocks)], axis=1)
    lam = lam_ref[...]
    neg_log_sig = jnp.maximum(-lam, 0.0) + jnp.log1p(jnp.exp(-jnp.abs(lam)))
    log_a = (-LRU_C) * jax.nn.sigmoid(ga + bga_ref[...]) * neg_log_sig
    a = jnp.exp(log_a)
    bt = jnp.sqrt(jnp.tanh(-log_a) * (1.0 + a * a)) * jax.nn.sigmoid(gx + bgx_ref[...]) * xc

    if stride == x_ref.shape[0]:
        h = a * hcar_scr[...] + bt
        hcar_scr[...] = h
    else:
        hcar_scr[...] = _scan_rows(a, bt, h_scr, hcar_scr[...])
        h = h_scr[...]
    hlast_ref[...] = hcar_scr[...]
    z = _dot(u, wz_ref[...])
    mix = _dot((h * _silu(z)).astype(BF16), wout_ref[...])
    y_ref[...] = _residual_ln(x_ref[...], mix, gate_ref[...], lng_ref[...], lnb_ref[...], alpha)


def _lru(x, shift, scale, gate, state, h0, w_in, w_conv, b_conv, w_ga, w_gx, b_ga, b_gx, lam, w_out, ln_g, ln_b, *,
         tm, stride, tiles_per_seq, alpha):
    rows, d = x.shape
    p = state.shape[1]
    hr = h0.shape[1]
    n_i = rows // tm
    gm = shift.shape[-2]
    if shift.ndim == 3:
        mod_spec = pl.BlockSpec((None, gm, d), lambda i: (i // tiles_per_seq, 0, 0))
    else:
        mod_spec = pl.BlockSpec((gm, d), lambda i: (0, 0))
    once = pl.Buffered(1)
    vec_spec = pl.BlockSpec((1, d), lambda i: (0, 0))
    gate_spec = pl.BlockSpec(w_ga.shape, lambda i: (0, 0, 0), pipeline_mode=once)
    scan_scr = (tm, d) if stride != tm else (SUBLANES, 128)
    return pl.pallas_call(
        functools.partial(_lru_kernel, stride=stride, tiles_per_seq=tiles_per_seq, alpha=alpha),
        grid=(n_i,),
        in_specs=[pl.BlockSpec((tm, d), lambda i: (i, 0)), mod_spec, mod_spec, mod_spec,
                  pl.BlockSpec((None, p, d), lambda i: (i // tiles_per_seq, 0, 0)),
                  pl.BlockSpec((None, hr, d), lambda i: (i // tiles_per_seq, 0, 0)),
                  pl.BlockSpec((d, d), lambda i: (0, 0), pipeline_mode=once),
                  pl.BlockSpec((d, d), lambda i: (0, 1), pipeline_mode=once),
                  pl.BlockSpec((w_conv.shape[0], d), lambda i: (0, 0)), vec_spec,
                  gate_spec, gate_spec, vec_spec, vec_spec, vec_spec,
                  pl.BlockSpec((d, d), lambda i: (0, 0), pipeline_mode=once), vec_spec, vec_spec],
        out_specs=[pl.BlockSpec((tm, d), lambda i: (i, 0)),
                   pl.BlockSpec((None, p, d), lambda i: (i, 0, 0)),
                   pl.BlockSpec((None, hr, d), lambda i: (i, 0, 0))],
        out_shape=[jax.ShapeDtypeStruct((rows, d), F32), jax.ShapeDtypeStruct((n_i, p, d), F32),
                   jax.ShapeDtypeStruct((n_i, hr, d), F32)],
        scratch_shapes=[pltpu.VMEM((p, d), F32), pltpu.VMEM((hr, d), F32), pltpu.VMEM(scan_scr, F32)],
        compiler_params=_cparams(1),
        name="lru",
    )(x, shift, scale, gate, state, h0, w_in, w_in, w_conv, b_conv.reshape(1, d), w_ga, w_gx,
      b_ga.reshape(1, d), b_gx.reshape(1, d), lam.reshape(1, d), w_out, ln_g.reshape(1, d), ln_b.reshape(1, d))


def kernel(x_prompt, x_sample, c_prompt, c_sample, cache_k, cache_v, state_conv_b, state_conv_c, state_h, page_table, we_ada, be_ada, we_in, we_sb_bias, we_conv, we_out, ge_ln, be_ln, wo_ada, bo_ada, wo_in, wo_conv, bo_conv, wo_gate_a, bo_gate_a, wo_gate_x, bo_gate_x, wo_lambda, wo_out, go_ln, bo_ln):
    nb, seq, d = x_prompt.shape
    ns, dec, _ = x_sample.shape
    assert we_ada.shape[0] == 1 and wo_ada.shape[0] == 1, "one even and one odd layer"
    n_heads = we_sb_bias.shape[1]
    width = n_heads * HEAD_DIM
    alpha = (2.0 * (we_ada.shape[0] + wo_ada.shape[0])) ** 0.25
    tm = 512
    tm_in0 = 1024

    def to_time_major(a):
        return jnp.swapaxes(a, 0, 1).reshape(a.shape[1] * ns, a.shape[2])

    def to_seq_major(a, t):
        return jnp.swapaxes(a.reshape(t, ns, a.shape[-1]), 0, 1)

    pad = (-(nb + ns)) % SUBLANES
    c_all = jnp.concatenate([c_prompt, c_sample, jnp.zeros((pad, d), F32)], axis=0)
    mods = []
    m_all = _ada(c_all, we_ada[0], be_ada[0], wo_ada[0], bo_ada[0])
    for m in (m_all[0], m_all[1]):
        parts = [m[:, g * d:(g + 1) * d] for g in range(3)]
        mods.append(([v[:nb].reshape(nb, 1, d) for v in parts], [v[nb:nb + ns] for v in parts]))

    xp = x_prompt.reshape(nb * seq, d)
    xs = to_time_major(x_sample)
    bias = we_sb_bias[0].reshape(1, n_heads)

    (shift_p, scale_p, gate_p), (shift_s, scale_s, gate_s) = mods[0]
    w_in0 = we_in[0].astype(BF16)
    w_out0 = we_out[0].astype(BF16)
    conv_k = we_conv.shape[1]
    qp, kp, vp, zap, obp, tailp = _inproj0(
        xp, shift_p, scale_p, jnp.zeros((nb, SUBLANES, width), F32), w_in0, we_conv[0],
        tm=tm_in0, tn=256, stride=1, tiles_per_seq=seq // tm_in0, q_dtype=BF16)
    conv_b_prompt = tailp.reshape(nb, seq // tm_in0, SUBLANES, width)[:, -1, SUBLANES - (conv_k - 1):][None]

    state_b = to_time_major(state_conv_b[0])[None]
    rows_s = ns * dec
    tiles_s = rows_s // tm
    qs, ks, vs, zas, obs, tails = _inproj0(
        xs, shift_s, scale_s, state_b, w_in0, we_conv[0],
        tm=tm, tn=256, stride=ns, tiles_per_seq=tiles_s, q_dtype=F32)
    k_sample = to_seq_major(ks, dec).reshape(1, ns, dec, n_heads, HEAD_DIM)
    v_sample = to_seq_major(vs, dec).reshape(1, ns, dec, n_heads, HEAD_DIM)
    n_phys = cache_k.shape[1]
    oas, oap = _attn(page_table, bias, to_seq_major(qs, dec), k_sample.reshape(ns, dec * n_heads, HEAD_DIM),
                     v_sample.reshape(ns, dec * n_heads, HEAD_DIM), to_seq_major(zas, dec),
                     cache_k.reshape(-1, HEAD_DIM), cache_v.reshape(-1, HEAD_DIM), qp, kp, vp, zap,
                     n_phys=n_phys, n_batch=nb, seq=seq)
    yp, w_in1, w_out1 = _outproj_ln(xp, gate_p, [oap, obp], w_out0, ge_ln[0], be_ln[0], tm=tm,
                                    tiles_per_seq=seq // tm, alpha=alpha, casts=(wo_in[0], wo_out[0]))
    ys = _outproj_ln(xs, gate_s, [to_time_major(oas), obs], w_out0, ge_ln[0], be_ln[0], tm=tm,
                     tiles_per_seq=tiles_s, alpha=alpha)
    conv_b_sample = to_seq_major(tails[-1], conv_k - 1)[None]

    (shift_p, scale_p, gate_p), (shift_s, scale_s, gate_s) = mods[1]
    w_ga = wo_gate_a[0].astype(BF16)
    w_gx = wo_gate_x[0].astype(BF16)
    conv_k = wo_conv.shape[1]
    lru_w = (w_in1, wo_conv[0], bo_conv[0], w_ga, w_gx, bo_gate_a[0], bo_gate_x[0], wo_lambda[0],
             w_out1, go_ln[0], bo_ln[0])
    tm1 = 256
    yp, tailp, hp = _lru(yp, shift_p, scale_p, gate_p, jnp.zeros((nb, SUBLANES, d), F32),
                         jnp.zeros((nb, 1, d), F32), *lru_w, tm=tm1, stride=1, tiles_per_seq=seq // tm1, alpha=alpha)
    conv_c_prompt = tailp.reshape(nb, seq // tm1, SUBLANES, d)[:, -1, SUBLANES - (conv_k - 1):][None]
    h_prompt = hp.reshape(nb, seq // tm1, d)[:, -1][None]

    state_c = to_time_major(state_conv_c[0])[None]
    ys, tails, hs = _lru(ys, shift_s, scale_s, gate_s, state_c, state_h[0][None], *lru_w, tm=ns, stride=ns,
                         tiles_per_seq=dec, alpha=alpha)
    conv_c_sample = to_seq_major(tails[-1], conv_k - 1)[None]
    h_sample = hs[-1][None]

    y_prompt = yp.reshape(nb, seq, d)
    y_sample = to_seq_major(ys, dec)
    k_prompt = kp.reshape(1, nb, seq, n_heads, HEAD_DIM)
    v_prompt = vp.reshape(1, nb, seq, n_heads, HEAD_DIM)
    return (y_prompt, y_sample, k_prompt, v_prompt, conv_b_prompt, conv_c_prompt, h_prompt,
            k_sample, v_sample, conv_b_sample, conv_c_sample, h_sample)
```
